```python
import jax, jax.numpy as jnp
from jax import lax
import numpy as np

D_MODEL = 2048
BATCH = 2
SEQ = 16384
DEPTH = 2

CHUNK = 64
SB_HEAD_DIM = 128
SB_HEADS = 4
SB_WIDTH = SB_HEADS * SB_HEAD_DIM
Q_BLOCK = 128
CV_WIDTH = D_MODEL // 4
CV_FILTER = 31
SG_WIDTH = D_MODEL // 4
SG_GROUPS = 8
SG_CHUNK = 128
N_BRANCH = 3
FFN_HIDDEN = 2 * D_MODEL
FFN_FILTER = 3
IN_COLS = 3 * SB_WIDTH + 2 * CV_WIDTH + 2 * SG_WIDTH
EPS = 1e-6

kernel_name = "hybrid_stickbreak_conformer_gmlp_block"


def rms_norm(x, g):
    xf = x.astype(jnp.float32)
    y = xf * lax.rsqrt(jnp.mean(xf * xf, axis=-1, keepdims=True) + EPS)
    return (y * g.astype(jnp.float32)).astype(x.dtype)


def layer_norm(x, g, b):
    xf = x.astype(jnp.float32)
    mu = jnp.mean(xf, axis=-1, keepdims=True)
    var = jnp.mean(jnp.square(xf - mu), axis=-1, keepdims=True)
    y = (xf - mu) * lax.rsqrt(var + EPS)
    return (y * g.astype(jnp.float32) + b.astype(jnp.float32)).astype(x.dtype)


def causal_depthwise_conv(x, w, b):
    k = w.shape[0]
    out = lax.conv_general_dilated(
        x, w[:, None, :].astype(x.dtype), window_strides=(1,), padding=[(k - 1, 0)],
        dimension_numbers=("NWC", "WIO", "NWC"), feature_group_count=x.shape[-1])
    return out + b.astype(x.dtype)


def stick_breaking_block(q_blk, k_all, v_all, blk):
    bsz, h, nq, dh = q_blk.shape
    n_keys = k_all.shape[2]
    nkb = n_keys // nq
    z = jnp.einsum("bhqd,bhkd->bhqk", q_blk, k_all).astype(jnp.float32) * (1.0 / float(np.sqrt(dh)))
    q_pos = blk * nq + jnp.arange(nq, dtype=jnp.int32)
    key_pos = jnp.arange(n_keys, dtype=jnp.int32)
    mask = key_pos[None, :] < q_pos[:, None]
    ls_neg = jax.nn.log_sigmoid(-z)
    l1m = jnp.where(mask, ls_neg, 0.0).reshape(bsz, h, nq, nkb, nq)
    upper = jnp.triu(jnp.ones((nq, nq), jnp.float32), k=1)
    upper_b = jnp.triu(jnp.ones((nkb, nkb), jnp.float32), k=1)
    within = jnp.einsum("bhqnj,sj->bhqns", l1m, upper)
    later = jnp.einsum("bhqm,nm->bhqn", jnp.sum(l1m, axis=-1), upper_b)
    suffix = (within + later[..., None]).reshape(bsz, h, nq, n_keys)
    w = jnp.where(mask, jnp.exp(z + ls_neg + suffix), 0.0)
    return jnp.einsum("bhqk,bhkd->bhqd", w.astype(v_all.dtype), v_all)


def stick_breaking_attention(q, k, v):
    bsz, s_len, h, dh = q.shape
    nb = s_len // Q_BLOCK
    qh = q.transpose(0, 2, 1, 3)
    kh = k.transpose(0, 2, 1, 3)
    vh = v.transpose(0, 2, 1, 3)
    outs = []
    for blk in range(nb):
        lo, hi = blk * Q_BLOCK, (blk + 1) * Q_BLOCK
        outs.append(stick_breaking_block(qh[:, :, lo:hi], kh[:, :, :hi], vh[:, :, :hi], blk))
    out = jnp.concatenate(outs, axis=2)
    return out.transpose(0, 2, 1, 3).reshape(bsz, s_len, h * dh)


def conformer_conv(a, b, w_dw, b_dw, ln_g, ln_b):
    y = a * jax.nn.sigmoid(b)
    y = causal_depthwise_conv(y, w_dw, b_dw)
    return jax.nn.silu(layer_norm(y, ln_g, ln_b))


def spatial_gating(u, v, ln_g, ln_b, w_s, b_s):
    bsz, s_len, width = v.shape
    cg = width // SG_GROUPS
    pos = jnp.arange(SG_CHUNK, dtype=jnp.int32) // CHUNK
    mask = (pos[None, :] <= pos[:, None]).astype(w_s.dtype)
    vn = layer_norm(v, ln_g, ln_b).reshape(bsz, s_len // SG_CHUNK, SG_CHUNK, SG_GROUPS, cg)
    mixed = jnp.einsum("gts,bnsgc->bntgc", w_s * mask[None], vn)
    mixed = mixed + jnp.transpose(b_s)[:, :, None]
    return u * mixed.reshape(bsz, s_len, width)


def setup_inputs(seed: int = 0) -> dict:
    key = jax.random.key(seed)
    ks = jax.random.split(key, 32)
    f32 = jnp.float32
    L, D = DEPTH, D_MODEL

    def nrm(k, shape, scale):
        return jax.random.normal(k, shape, f32) * scale

    return {
        "x": nrm(ks[0], (BATCH, SEQ, D), 1.0),
        "c": nrm(ks[1], (BATCH, D), 1.0),
        "w_ada": nrm(ks[2], (L, D, 6 * D), 0.5 * D ** -0.5),
        "b_ada": nrm(ks[3], (L, 6 * D), 0.01),
        "g_norm1": 1.0 + nrm(ks[4], (L, D), 0.02),
        "g_norm2": 1.0 + nrm(ks[5], (L, D), 0.02),
        "w_in": nrm(ks[6], (L, D, IN_COLS), D ** -0.5),
        "sb_w_out": nrm(ks[7], (L, SB_WIDTH, D), SB_WIDTH ** -0.5),
        "cv_w_dw": nrm(ks[8], (L, CV_FILTER, CV_WIDTH), CV_FILTER ** -0.5),
        "cv_b_dw": nrm(ks[9], (L, CV_WIDTH), 0.01),
        "cv_ln_g": 1.0 + nrm(ks[10], (L, CV_WIDTH), 0.02),
        "cv_ln_b": nrm(ks[11], (L, CV_WIDTH), 0.01),
        "cv_w_out": nrm(ks[12], (L, CV_WIDTH, D), CV_WIDTH ** -0.5),
        "sg_ln_g": 1.0 + nrm(ks[13], (L, SG_WIDTH), 0.02),
        "sg_ln_b": nrm(ks[14], (L, SG_WIDTH), 0.01),
        "sg_w_s": nrm(ks[15], (L, SG_GROUPS, SG_CHUNK, SG_CHUNK), SG_CHUNK ** -0.5),
        "sg_b_s": 1.0 + nrm(ks[16], (L, SG_GROUPS, SG_CHUNK), 0.02),
        "sg_w_out": nrm(ks[17], (L, SG_WIDTH, D), SG_WIDTH ** -0.5),
        "w_gate": nrm(ks[18], (L, D, N_BRANCH * D), D ** -0.5),
        "b_gate": nrm(ks[19], (L, N_BRANCH * D), 0.01),
        "w_o": nrm(ks[20], (L, D, D), D ** -0.5),
        "ffn_w_up": nrm(ks[21], (L, D, 2 * FFN_HIDDEN), D ** -0.5),
        "ffn_w_dw": nrm(ks[22], (L, FFN_FILTER, FFN_HIDDEN), FFN_FILTER ** -0.5),
        "ffn_b_dw": nrm(ks[23], (L, FFN_HIDDEN), 0.01),
        "ffn_w_down": nrm(ks[24], (L, FFN_HIDDEN, D), FFN_HIDDEN ** -0.5),
        "g_final": 1.0 + nrm(ks[25], (D,), 0.02),
    }


def reference(x, c, w_ada, b_ada, g_norm1, g_norm2, w_in, sb_w_out, cv_w_dw, cv_b_dw,
              cv_ln_g, cv_ln_b, cv_w_out, sg_ln_g, sg_ln_b, sg_w_s, sg_b_s, sg_w_out,
              w_gate, b_gate, w_o, ffn_w_up, ffn_w_dw, ffn_b_dw, ffn_w_down, g_final):
    bsz, s_len, d = x.shape
    splits = [SB_WIDTH, 2 * SB_WIDTH, 3 * SB_WIDTH, 3 * SB_WIDTH + CV_WIDTH,
              3 * SB_WIDTH + 2 * CV_WIDTH, 3 * SB_WIDTH + 2 * CV_WIDTH + SG_WIDTH]
    cond = jax.nn.silu(c)
    for l in range(DEPTH):
        mod = (cond @ w_ada[l] + b_ada[l])[:, None, :]
        sh1, sc1, gt1, sh2, sc2, gt2 = jnp.split(mod, 6, axis=-1)

        h = rms_norm(x, g_norm1[l]) * (1.0 + sc1) + sh1
        proj = h @ w_in[l]
        q, k, v, cv_a, cv_b, sg_u, sg_v = jnp.split(proj, splits, axis=-1)

        hd = (bsz, s_len, SB_HEADS, SB_HEAD_DIM)
        y_a = stick_breaking_attention(q.reshape(hd), k.reshape(hd), v.reshape(hd)) @ sb_w_out[l]
        y_b = conformer_conv(cv_a, cv_b, cv_w_dw[l], cv_b_dw[l], cv_ln_g[l], cv_ln_b[l]) @ cv_w_out[l]
        y_c = spatial_gating(jax.nn.gelu(sg_u), jax.nn.gelu(sg_v), sg_ln_g[l], sg_ln_b[l],
                             sg_w_s[l], sg_b_s[l]) @ sg_w_out[l]

        gates = jax.nn.sigmoid(h @ w_gate[l] + b_gate[l]).reshape(bsz, s_len, N_BRANCH, d)
        merged = gates[:, :, 0] * y_a + gates[:, :, 1] * y_b + gates[:, :, 2] * y_c
        x = x + gt1 * (merged @ w_o[l])

        h2 = rms_norm(x, g_norm2[l]) * (1.0 + sc2) + sh2
        gate_pre, val = jnp.split(h2 @ ffn_w_up[l], 2, axis=-1)
        act = jax.nn.silu(causal_depthwise_conv(gate_pre, ffn_w_dw[l], ffn_b_dw[l])) * val
        x = x + gt2 * (act @ ffn_w_down[l])

    return rms_norm(x, g_final)
```

```python
import functools
import math

import jax
import jax.numpy as jnp
from jax import lax
from jax.experimental import pallas as pl
from jax.experimental.pallas import tpu as pltpu

F32 = jnp.float32
BF16 = jnp.bfloat16

EPS = 1e-6
SB_HEADS = 4
SB_HEAD_DIM = 128
SB_WIDTH = SB_HEADS * SB_HEAD_DIM
BRANCH_WIDTH = 512
CV_FILTER = 31
SG_GROUPS = 8
SG_CHUNK = 128
SG_CAUSAL_CHUNK = 64
FFN_FILTER = 3

V7X_VMEM_BYTES = 64 * 1024 * 1024
V7X_LANES = 128
V7X_SUBLANES_F32 = 8
V7X_SUBLANES_BF16 = 16

SB_UNDERFLOW_LOG2 = 150.0

CV_HALO = 2 * V7X_SUBLANES_BF16
FFN_HALO = V7X_SUBLANES_F32


def _vmem_limit(*nbytes):
    need = sum(nbytes)
    return int(min(need + need // 4, V7X_VMEM_BYTES - 4 * 1024 * 1024))


def _rms_modulate(x, g, shift, scale):
    ms = jnp.mean(x * x, axis=-1, keepdims=True)
    y = x * lax.rsqrt(ms + EPS) * g
    return y * (1.0 + scale) + shift


def _layer_norm(x, g, b):
    mu = jnp.mean(x, axis=-1, keepdims=True)
    xc = x - mu
    var = jnp.mean(xc * xc, axis=-1, keepdims=True)
    return xc * lax.rsqrt(var + EPS) * g + b


def _mod_kernel(c_ref, w_ref, b_ref, o_ref):
    c = c_ref[...]
    cond = (c * jax.nn.sigmoid(c)).astype(BF16)
    o_ref[...] = jnp.dot(cond, w_ref[...].astype(BF16), preferred_element_type=F32) + b_ref[...]


def _ada_mod(c, w_ada, b_ada):
    depth, d, cols = w_ada.shape
    bsz = c.shape[0]
    rows = V7X_SUBLANES_BF16
    tn = 1536
    c_pad = jnp.pad(c, ((0, rows - bsz), (0, 0)))
    out = pl.pallas_call(
        _mod_kernel,
        grid=(depth, cols // tn),
        in_specs=[
            pl.BlockSpec((rows, d), lambda l, j: (0, 0)),
            pl.BlockSpec((None, d, tn), lambda l, j: (l, 0, j)),
            pl.BlockSpec((None, 1, tn), lambda l, j: (l, 0, j)),
        ],
        out_specs=pl.BlockSpec((None, rows, tn), lambda l, j: (l, 0, j)),
        out_shape=jax.ShapeDtypeStruct((depth, rows, cols), F32),
        compiler_params=pltpu.CompilerParams(
            dimension_semantics=("parallel", "parallel"),
            vmem_limit_bytes=_vmem_limit(2 * d * tn * 4, d * tn * 2)),
    )(c_pad, w_ada, b_ada.reshape(depth, 1, cols))
    return out[:, :bsz].reshape(depth, bsz, 6, d)


def _inproj_kernel(x_ref, mod_ref, g_ref, w_ref, proj_ref, h_ref):
    h = _rms_modulate(x_ref[...], g_ref[...], mod_ref[0:1, :], mod_ref[1:2, :]).astype(BF16)
    h_ref[...] = h
    proj_ref[...] = jnp.dot(h, w_ref[...], preferred_element_type=F32).astype(BF16)


def _inproj(x, mod, g, w, *, seq, tm=512):
    n, d = x.shape
    cols = w.shape[1]
    tps = seq // tm
    return pl.pallas_call(
        _inproj_kernel,
        grid=(n // tm,),
        in_specs=[
            pl.BlockSpec((tm, d), lambda i: (i, 0)),
            pl.BlockSpec((None, 6, d), lambda i: (i // tps, 0, 0)),
            pl.BlockSpec((1, d), lambda i: (0, 0)),
            pl.BlockSpec((d, cols), lambda i: (0, 0), pipeline_mode=pl.Buffered(1)),
        ],
        out_specs=[
            pl.BlockSpec((tm, cols), lambda i: (i, 0)),
            pl.BlockSpec((tm, d), lambda i: (i, 0)),
        ],
        out_shape=[
            jax.ShapeDtypeStruct((n, cols), BF16),
            jax.ShapeDtypeStruct((n, d), BF16),
        ],
        compiler_params=pltpu.CompilerParams(
            dimension_semantics=("parallel",),
            vmem_limit_bytes=_vmem_limit(2 * tm * d * 4, d * cols * 2, 2 * tm * cols * 2,
                                         2 * tm * d * 2, tm * cols * 4, tm * d * 4)),
    )(x, mod, g.reshape(1, d), w)


def _attn_kernel(q_ref, k_ref, v_ref, o_ref, acc_scr, carry_scr, *, tq):
    qi = pl.program_id(2)
    q = q_ref[...]
    row = lax.broadcasted_iota(jnp.int32, (tq, tq), 0)
    col = lax.broadcasted_iota(jnp.int32, (tq, tq), 1)
    later = jnp.where(row > col, 1.0, 0.0).astype(BF16)
    causal = col < row

    def sweep_block(kb, diagonal):
        start = pl.multiple_of(kb * tq, tq)
        k = k_ref[pl.ds(start, tq), :]
        v = v_ref[pl.ds(start, tq), :]
        z = lax.dot_general(q, k, (((1,), (1,)), ((), ())), preferred_element_type=F32)
        neg_log1m = jnp.maximum(z, 0.0) + jnp.log2(1.0 + jnp.exp2(-jnp.abs(z)))
        log_beta = z - neg_log1m
        if diagonal:
            neg_log1m = jnp.where(causal, neg_log1m, 0.0)
        within = jnp.dot(neg_log1m.astype(BF16), later, preferred_element_type=F32)
        carry = carry_scr[...]
        w = jnp.exp2(log_beta - (within + carry))
        if diagonal:
            w = jnp.where(causal, w, 0.0)
        acc_scr[...] += jnp.dot(w.astype(BF16), v, preferred_element_type=F32)
        carry_scr[...] = carry + jnp.sum(neg_log1m, axis=-1, keepdims=True)

    def stick_left():
        return jnp.min(carry_scr[...]) < SB_UNDERFLOW_LOG2

    acc_scr[...] = jnp.zeros_like(acc_scr)
    carry_scr[...] = jnp.zeros_like(carry_scr)
    sweep_block(qi, True)

    def cond(state):
        kb, go = state
        return jnp.logical_and(kb >= 0, go)

    def body(state):
        kb, _ = state
        sweep_block(kb, False)
        return kb - 1, stick_left()

    lax.while_loop(cond, body, (qi - 1, stick_left()))
    o_ref[...] = acc_scr[...].astype(BF16)


def _attention(proj, *, bsz, seq, tq=256):
    cols = proj.shape[1]
    proj3 = proj.reshape(bsz, seq, cols)
    dh = SB_HEAD_DIM
    out = pl.pallas_call(
        functools.partial(_attn_kernel, tq=tq),
        grid=(bsz, SB_HEADS, seq // tq),
        in_specs=[
            pl.BlockSpec((None, tq, dh), lambda b, h, i: (b, i, h)),
            pl.BlockSpec((None, seq, dh), lambda b, h, i: (b, 0, SB_HEADS + h)),
            pl.BlockSpec((None, seq, dh), lambda b, h, i: (b, 0, 2 * SB_HEADS + h)),
        ],
        out_specs=pl.BlockSpec((None, tq, dh), lambda b, h, i: (b, i, h)),
        out_shape=jax.ShapeDtypeStruct((bsz, seq, SB_WIDTH), BF16),
        scratch_shapes=[pltpu.VMEM((tq, dh), F32), pltpu.VMEM((tq, 1), F32)],
        compiler_params=pltpu.CompilerParams(
            dimension_semantics=("parallel", "parallel", "arbitrary"),
            vmem_limit_bytes=_vmem_limit(4 * seq * dh * 2, 4 * tq * dh * 2, 8 * tq * tq * 4)),
    )(proj3, proj3, proj3)
    return out.reshape(bsz * seq, SB_WIDTH)


def _cv_kernel(a_ref, b_ref, ah_ref, bh_ref, wdw_ref, bdw_ref, lg_ref, lb_ref, o_ref, pad_scr,
               *, tm, tps):
    i = pl.program_id(0)
    pad_scr[CV_HALO:, :] = a_ref[...].astype(F32) * jax.nn.sigmoid(b_ref[...].astype(F32))

    @pl.when(i % tps == 0)
    def _():
        pad_scr[:CV_HALO, :] = jnp.zeros((CV_HALO, BRANCH_WIDTH), F32)

    @pl.when(i % tps != 0)
    def _():
        pad_scr[:CV_HALO, :] = ah_ref[...].astype(F32) * jax.nn.sigmoid(bh_ref[...].astype(F32))

    acc = jnp.broadcast_to(bdw_ref[...], (tm, BRANCH_WIDTH))
    for tap in range(CV_FILTER):
        off = CV_HALO - (CV_FILTER - 1) + tap
        acc = acc + wdw_ref[tap:tap + 1, :] * pad_scr[off:off + tm, :]
    y = _layer_norm(acc, lg_ref[...], lb_ref[...])
    o_ref[...] = (y * jax.nn.sigmoid(y)).astype(BF16)


def _conformer(proj, w_dw, b_dw, ln_g, ln_b, *, seq, tm=256):
    n = proj.shape[0]
    w = BRANCH_WIDTH
    tps = seq // tm
    col_a = 3 * SB_WIDTH // w
    col_b = col_a + 1
    hb = tm // CV_HALO
    row = lambda v: v.reshape(1, w)
    return pl.pallas_call(
        functools.partial(_cv_kernel, tm=tm, tps=tps),
        grid=(n // tm,),
        in_specs=[
            pl.BlockSpec((tm, w), lambda i: (i, col_a)),
            pl.BlockSpec((tm, w), lambda i: (i, col_b)),
            pl.BlockSpec((CV_HALO, w), lambda i: (jnp.maximum(i * hb - 1, 0), col_a)),
            pl.BlockSpec((CV_HALO, w), lambda i: (jnp.maximum(i * hb - 1, 0), col_b)),
            pl.BlockSpec((CV_FILTER, w), lambda i: (0, 0)),
            pl.BlockSpec((1, w), lambda i: (0, 0)),
            pl.BlockSpec((1, w), lambda i: (0, 0)),
            pl.BlockSpec((1, w), lambda i: (0, 0)),
        ],
        out_specs=pl.BlockSpec((tm, w), lambda i: (i, 0)),
        out_shape=jax.ShapeDtypeStruct((n, w), BF16),
        scratch_shapes=[pltpu.VMEM((tm + CV_HALO, w), F32)],
        compiler_params=pltpu.CompilerParams(
            dimension_semantics=("parallel",),
            vmem_limit_bytes=_vmem_limit(6 * tm * w * 2, 8 * tm * w * 4)),
    )(proj, proj, proj, proj, w_dw, row(b_dw), row(ln_g), row(ln_b))


def _sg_kernel(u_ref, v_ref, lg_ref, lb_ref, ws_ref, bs_ref, o_ref, *, tm):
    t = SG_CHUNK
    u = jax.nn.gelu(u_ref[...].astype(F32))
    v = jax.nn.gelu(v_ref[...].astype(F32))
    vn = _layer_norm(v, lg_ref[...], lb_ref[...]).astype(BF16)
    pos_t = lax.broadcasted_iota(jnp.int32, (t, t), 0) // SG_CAUSAL_CHUNK
    pos_s = lax.broadcasted_iota(jnp.int32, (t, t), 1) // SG_CAUSAL_CHUNK
    chunk_causal = pos_s <= pos_t
    ws = [jnp.where(chunk_causal, ws_ref[g], 0.0).astype(BF16) for g in range(SG_GROUPS)]
    cg = BRANCH_WIDTH // SG_GROUPS
    first_group = lax.broadcasted_iota(jnp.int32, (t, V7X_LANES), 1) < cg
    for c in range(tm // t):
        rows = slice(c * t, (c + 1) * t)
        for p in range(BRANCH_WIDTH // V7X_LANES):
            cols = slice(p * V7X_LANES, (p + 1) * V7X_LANES)
            vb = vn[rows, cols]
            r0 = jnp.dot(ws[2 * p], vb, preferred_element_type=F32)
            r1 = jnp.dot(ws[2 * p + 1], vb, preferred_element_type=F32)
            mixed = jnp.where(first_group, r0, r1) + bs_ref[:, cols]
            o_ref[rows, cols] = (u[rows, cols] * mixed).astype(BF16)


def _spatial(proj, ln_g, ln_b, w_s, b_s, *, tm=512):
    n = proj.shape[0]
    w = BRANCH_WIDTH
    col_u = (3 * SB_WIDTH + 2 * w) // w
    col_v = col_u + 1
    bias = jnp.repeat(b_s.T, w // SG_GROUPS, axis=1)
    row = lambda v: v.reshape(1, w)
    return pl.pallas_call(
        functools.partial(_sg_kernel, tm=tm),
        grid=(n // tm,),
        in_specs=[
            pl.BlockSpec((tm, w), lambda i: (i, col_u)),
            pl.BlockSpec((tm, w), lambda i: (i, col_v)),
            pl.BlockSpec((1, w), lambda i: (0, 0)),
            pl.BlockSpec((1, w), lambda i: (0, 0)),
            pl.BlockSpec((SG_GROUPS, SG_CHUNK, SG_CHUNK), lambda i: (0, 0, 0)),
            pl.BlockSpec((SG_CHUNK, w), lambda i: (0, 0)),
        ],
        out_specs=pl.BlockSpec((tm, w), lambda i: (i, 0)),
        out_shape=jax.ShapeDtypeStruct((n, w), BF16),
        compiler_params=pltpu.CompilerParams(
            dimension_semantics=("parallel",),
            vmem_limit_bytes=_vmem_limit(6 * tm * w * 2, 8 * tm * w * 4)),
    )(proj, proj, row(ln_g), row(ln_b), w_s, bias)


def _merge_kernel(x_ref, h_ref, mod_ref, ya_ref, yb_ref, yc_ref,
                  wga_ref, wgb_ref, wgc_ref, bga_ref, bgb_ref, bgc_ref,
                  woa_ref, wob_ref, woc_ref, wo_ref, out_ref):
    j = pl.program_id(1)

    @pl.when(j == 0)
    def _():
        out_ref[...] = jnp.zeros_like(out_ref)

    h = h_ref[...]
    merged = None
    for y_ref, wg_ref, bg_ref, wout_ref in ((ya_ref, wga_ref, bga_ref, woa_ref),
                                            (yb_ref, wgb_ref, bgb_ref, wob_ref),
                                            (yc_ref, wgc_ref, bgc_ref, woc_ref)):
        gate = jax.nn.sigmoid(jnp.dot(h, wg_ref[...], preferred_element_type=F32) + bg_ref[...])
        term = gate * jnp.dot(y_ref[...], wout_ref[...], preferred_element_type=F32)
        merged = term if merged is None else merged + term
    out_ref[...] += jnp.dot(merged.astype(BF16), wo_ref[...], preferred_element_type=F32)

    @pl.when(j == pl.num_programs(1) - 1)
    def _():
        out_ref[...] = x_ref[...] + mod_ref[2:3, :] * out_ref[...]


def _merge(x, h, mod, ya, yb, yc, w_gate, b_gate, wo_a, wo_b, wo_c, w_o, *, seq, tm=512, tn=512):
    n, d = x.shape
    w = BRANCH_WIDTH
    tps = seq // tm
    nj = d // tn
    bg = b_gate.reshape(1, 3 * d)
    gate_w = [pl.BlockSpec((d, tn), lambda i, j, k=k: (0, k * nj + j)) for k in range(3)]
    gate_b = [pl.BlockSpec((1, tn), lambda i, j, k=k: (0, k * nj + j)) for k in range(3)]
    branch = pl.BlockSpec((tm, w), lambda i, j: (i, 0))
    branch_w = pl.BlockSpec((w, tn), lambda i, j: (0, j))
    return pl.pallas_call(
        _merge_kernel,
        grid=(n // tm, nj),
        in_specs=[
            pl.BlockSpec((tm, d), lambda i, j: (i, 0)),
            pl.BlockSpec((tm, d), lambda i, j: (i, 0)),
            pl.BlockSpec((None, 6, d), lambda i, j: (i // tps, 0, 0)),
            branch, branch, branch,
            *gate_w, *gate_b,
            branch_w, branch_w, branch_w,
            pl.BlockSpec((tn, d), lambda i, j: (j, 0)),
        ],
        out_specs=pl.BlockSpec((tm, d), lambda i, j: (i, 0)),
        out_shape=jax.ShapeDtypeStruct((n, d), F32),
        compiler_params=pltpu.CompilerParams(
            dimension_semantics=("parallel", "arbitrary"),
            vmem_limit_bytes=_vmem_limit(4 * tm * d * 4, 2 * tm * d * 2, 6 * tm * w * 2,
                                         2 * 3 * d * tn * 2, 2 * 3 * w * tn * 2, 2 * tn * d * 2,
                                         6 * tm * tn * 4)),
    )(x, h, mod, ya, yb, yc, w_gate, w_gate, w_gate, bg, bg, bg, wo_a, wo_b, wo_c, w_o)


def _ffn_kernel(x_ref, mod_ref, g_ref, wg_ref, wv_ref, wdw_ref, bdw_ref, wd_ref, gfin_ref, out_ref,
                h_scr, ext_scr, tail_scr, *, tm, tps, final_norm):
    i = pl.program_id(0)
    j = pl.program_id(1)

    @pl.when(j == 0)
    def _():
        h_scr[...] = _rms_modulate(x_ref[...], g_ref[...], mod_ref[3:4, :], mod_ref[4:5, :]).astype(BF16)
        out_ref[...] = jnp.zeros_like(out_ref)

    h = h_scr[...]
    gate_pre = jnp.dot(h, wg_ref[...], preferred_element_type=F32)
    val = jnp.dot(h, wv_ref[...], preferred_element_type=F32)

    @pl.when(i % tps == 0)
    def _():
        ext_scr[:FFN_HALO, :] = jnp.zeros((FFN_HALO, gate_pre.shape[1]), F32)

    @pl.when(i % tps != 0)
    def _():
        ext_scr[:FFN_HALO, :] = tail_scr[j]

    ext_scr[FFN_HALO:, :] = gate_pre
    tail_scr[j] = gate_pre[tm - FFN_HALO:, :]
    conv = bdw_ref[...] + wdw_ref[FFN_FILTER - 1:FFN_FILTER, :] * gate_pre
    for tap in range(FFN_FILTER - 1):
        off = FFN_HALO - (FFN_FILTER - 1) + tap
        conv = conv + wdw_ref[tap:tap + 1, :] * ext_scr[off:off + tm, :]
    act = conv * jax.nn.sigmoid(conv) * val
    out_ref[...] += jnp.dot(act.astype(BF16), wd_ref[...], preferred_element_type=F32)

    @pl.when(j == pl.num_programs(1) - 1)
    def _():
        y = x_ref[...] + mod_ref[5:6, :] * out_ref[...]
        if final_norm:
            ms = jnp.mean(y * y, axis=-1, keepdims=True)
            y = y * lax.rsqrt(ms + EPS) * gfin_ref[...]
        out_ref[...] = y


def _ffn(x, mod, g, w_up, w_dw, b_dw, w_down, g_final, *, seq, final_norm, tm=512, th=512):
    n, d = x.shape
    f = w_down.shape[0]
    tps = seq // tm
    nj = f // th
    return pl.pallas_call(
        functools.partial(_ffn_kernel, tm=tm, tps=tps, final_norm=final_norm),
        grid=(n // tm, nj),
        in_specs=[
            pl.BlockSpec((tm, d), lambda i, j: (i, 0)),
            pl.BlockSpec((None, 6, d), lambda i, j: (i // tps, 0, 0)),
            pl.BlockSpec((1, d), lambda i, j: (0, 0)),
            pl.BlockSpec((d, th), lambda i, j: (0, j)),
            pl.BlockSpec((d, th), lambda i, j: (0, nj + j)),
            pl.BlockSpec((FFN_FILTER, th), lambda i, j: (0, j)),
            pl.BlockSpec((1, th), lambda i, j: (0, j)),
            pl.BlockSpec((th, d), lambda i, j: (j, 0)),
            pl.BlockSpec((1, d), lambda i, j: (0, 0)),
        ],
        out_specs=pl.BlockSpec((tm, d), lambda i, j: (i, 0)),
        out_shape=jax.ShapeDtypeStruct((n, d), F32),
        scratch_shapes=[
            pltpu.VMEM((tm, d), BF16),
            pltpu.VMEM((tm + FFN_HALO, th), F32),
            pltpu.VMEM((nj, FFN_HALO, th), F32),
        ],
        compiler_params=pltpu.CompilerParams(
            dimension_semantics=("arbitrary", "arbitrary"),
            vmem_limit_bytes=_vmem_limit(4 * tm * d * 4, tm * d * 2, 2 * 2 * d * th * 2,
                                         2 * th * d * 2, 8 * tm * th * 4)),
    )(x, mod, g.reshape(1, d), w_up, w_up, w_dw, b_dw.reshape(1, f), w_down, g_final.reshape(1, d))


def kernel(x, c, w_ada, b_ada, g_norm1, g_norm2, w_in, sb_w_out, cv_w_dw, cv_b_dw, cv_ln_g, cv_ln_b,
           cv_w_out, sg_ln_g, sg_ln_b, sg_w_s, sg_b_s, sg_w_out, w_gate, b_gate, w_o, ffn_w_up,
           ffn_w_dw, ffn_b_dw, ffn_w_down, g_final):
    bsz, seq, d = x.shape
    depth = w_in.shape[0]
    in_cols = w_in.shape[2]
    xf = x.reshape(bsz * seq, d)
    mod = _ada_mod(c, w_ada, b_ada)

    q_scale = math.log2(math.e) / math.sqrt(SB_HEAD_DIM)
    col_scale = jnp.where(jnp.arange(in_cols) < SB_WIDTH, q_scale, 1.0).astype(F32)

    for l in range(depth):
        w_in_l = (w_in[l] * col_scale).astype(BF16)
        proj, h = _inproj(xf, mod[l], g_norm1[l], w_in_l, seq=seq)
        y_a = _attention(proj, bsz=bsz, seq=seq)
        y_b = _conformer(proj, cv_w_dw[l], cv_b_dw[l], cv_ln_g[l], cv_ln_b[l], seq=seq)
        y_c = _spatial(proj, sg_ln_g[l], sg_ln_b[l], sg_w_s[l], sg_b_s[l])
        xf = _merge(xf, h, mod[l], y_a, y_b, y_c, w_gate[l].astype(BF16), b_gate[l],
                    sb_w_out[l].astype(BF16), cv_w_out[l].astype(BF16), sg_w_out[l].astype(BF16),
                    w_o[l].astype(BF16), seq=seq)
        xf = _ffn(xf, mod[l], g_norm2[l], ffn_w_up[l].astype(BF16), ffn_w_dw[l], ffn_b_dw[l],
                  ffn_w_down[l].astype(BF16), g_final, seq=seq, final_norm=(l == depth - 1))
    return xf.reshape(bsz, seq, d)
```

```python
import functools
import math

import jax
import jax.numpy as jnp
from jax import lax
from jax.experimental import pallas as pl
from jax.experimental.pallas import tpu as pltpu

F32 = jnp.float32
BF16 = jnp.bfloat16

EPS = 1e-6
SB_HEADS = 4
SB_HEAD_DIM = 128
SB_WIDTH = SB_HEADS * SB_HEAD_DIM
BRANCH_WIDTH = 512
CV_FILTER = 31
SG_GROUPS = 8
SG_CHUNK = 128
SG_CAUSAL_CHUNK = 64
FFN_FILTER = 3

V7X_VMEM_BYTES = 64 * 1024 * 1024
V7X_LANES = 128
V7X_SUBLANES_F32 = 8
V7X_SUBLANES_BF16 = 16
MXU_COLS = 256

SB_UNDERFLOW_LOG2 = 150.0

CV_HALO = 2 * V7X_SUBLANES_BF16
CV_ROWS = 32
FFN_HALO = V7X_SUBLANES_F32

SHIFT1, SCALE1, GATE1, SHIFT2, SCALE2, GATE2 = range(6)


def _vmem_limit(*nbytes):
    need = sum(nbytes)
    return int(min(need + need // 4, V7X_VMEM_BYTES - 4 * 1024 * 1024))


def _rms_norm(x, g):
    ms = jnp.mean(x * x, axis=-1, keepdims=True)
    return x * lax.rsqrt(ms + EPS) * g


def _rms_modulate(x, g, shift, scale):
    return _rms_norm(x, g) * (1.0 + scale) + shift


def _layer_norm(x, g, b):
    mu = jnp.mean(x, axis=-1, keepdims=True)
    xc = x - mu
    var = jnp.mean(xc * xc, axis=-1, keepdims=True)
    return xc * lax.rsqrt(var + EPS) * g + b


def _mod_kernel(c_ref, w_ref, b_ref, o_ref):
    c = c_ref[...]
    cond = (c * jax.nn.sigmoid(c)).astype(BF16)
    o_ref[...] = jnp.dot(cond, w_ref[...].astype(BF16), preferred_element_type=F32) + b_ref[...]


def _ada_mod(c, w_ada, b_ada):
    depth, d, cols = w_ada.shape
    bsz = c.shape[0]
    rows = V7X_SUBLANES_BF16
    tn = 1536
    c_pad = jnp.pad(c, ((0, rows - bsz), (0, 0)))
    out = pl.pallas_call(
        _mod_kernel,
        grid=(depth, cols // tn),
        in_specs=[
            pl.BlockSpec((rows, d), lambda l, j: (0, 0)),
            pl.BlockSpec((None, d, tn), lambda l, j: (l, 0, j)),
            pl.BlockSpec((None, 1, tn), lambda l, j: (l, 0, j)),
        ],
        out_specs=pl.BlockSpec((None, rows, tn), lambda l, j: (l, 0, j)),
        out_shape=jax.ShapeDtypeStruct((depth, rows, cols), F32),
        compiler_params=pltpu.CompilerParams(
            dimension_semantics=("parallel", "parallel"),
            vmem_limit_bytes=_vmem_limit(2 * d * tn * 4, d * tn * 2)),
    )(c_pad, w_ada, b_ada.reshape(depth, 1, cols))
    return out[:, :bsz].reshape(depth, bsz, 6, d)


def _inproj_kernel(x_ref, mod_ref, g_ref, w_ref, proj_ref, h_ref, h_even, h_odd):
    s = pl.program_id(0)

    @pl.when(s == 0)
    def _():
        h_odd[...] = jnp.zeros_like(h_odd)

    def step(h_cur, h_next):
        h = h_cur[...]
        h_ref[...] = h
        proj_ref[...] = jnp.dot(h, w_ref[...], preferred_element_type=F32).astype(BF16)
        h_next[...] = _rms_modulate(x_ref[...], g_ref[...], mod_ref[SHIFT1:SHIFT1 + 1, :],
                                    mod_ref[SCALE1:SCALE1 + 1, :]).astype(BF16)

    @pl.when(s % 2 == 0)
    def _():
        step(h_odd, h_even)

    @pl.when(s % 2 == 1)
    def _():
        step(h_even, h_odd)


def _inproj(x, mod, g, w, *, seq, tm=512):
    n, d = x.shape
    cols = w.shape[1]
    tps = seq // tm
    nt = n // tm
    cur = lambda s: jnp.minimum(s, nt - 1)
    prev = lambda s: jnp.maximum(s - 1, 0)
    return pl.pallas_call(
        _inproj_kernel,
        grid=(nt + 1,),
        in_specs=[
            pl.BlockSpec((tm, d), lambda s: (cur(s), 0)),
            pl.BlockSpec((None, 6, d), lambda s: (cur(s) // tps, 0, 0)),
            pl.BlockSpec((1, d), lambda s: (0, 0)),
            pl.BlockSpec((d, cols), lambda s: (0, 0), pipeline_mode=pl.Buffered(1)),
        ],
        out_specs=[
            pl.BlockSpec((tm, cols), lambda s: (prev(s), 0)),
            pl.BlockSpec((tm, d), lambda s: (prev(s), 0)),
        ],
        out_shape=[
            jax.ShapeDtypeStruct((n, cols), BF16),
            jax.ShapeDtypeStruct((n, d), BF16),
        ],
        scratch_shapes=[pltpu.VMEM((tm, d), BF16), pltpu.VMEM((tm, d), BF16)],
        compiler_params=pltpu.CompilerParams(
            dimension_semantics=("arbitrary",),
            vmem_limit_bytes=_vmem_limit(2 * tm * d * 4, d * cols * 2, 2 * tm * cols * 2,
                                         4 * tm * d * 2, tm * cols * 4)),
    )(x, mod, g.reshape(1, d), w)


def _attn_kernel(q_ref, k_ref, v_ref, o_ref, acc_scr, carry_scr, *, tq):
    qi = pl.program_id(1)
    row = lax.broadcasted_iota(jnp.int32, (tq, tq), 0)
    col = lax.broadcasted_iota(jnp.int32, (tq, tq), 1)
    later = jnp.where(row > col, 1.0, 0.0).astype(BF16)
    causal = col < row

    def sweep_block(kb, diagonal):
        start = pl.multiple_of(kb * tq, tq)
        for hd in range(SB_HEADS):
            cols = slice(hd * SB_HEAD_DIM, (hd + 1) * SB_HEAD_DIM)
            q = q_ref[:, cols]
            k = k_ref[pl.ds(start, tq), cols]
            v = v_ref[pl.ds(start, tq), cols]
            z = lax.dot_general(q, k, (((1,), (1,)), ((), ())), preferred_element_type=F32)
            neg_log1m = jnp.maximum(z, 0.0) + jnp.log2(1.0 + jnp.exp2(-jnp.abs(z)))
            log_beta = z - neg_log1m
            if diagonal:
                neg_log1m = jnp.where(causal, neg_log1m, 0.0)
            within = jnp.dot(neg_log1m.astype(BF16), later, preferred_element_type=F32)
            carry = carry_scr[hd]
            w = jnp.exp2(log_beta - (within + carry))
            if diagonal:
                w = jnp.where(causal, w, 0.0)
            acc_scr[hd] += jnp.dot(w.astype(BF16), v, preferred_element_type=F32)
            carry_scr[hd] = carry + jnp.sum(neg_log1m, axis=-1, keepdims=True)

    def stick_left():
        return jnp.min(carry_scr[...]) < SB_UNDERFLOW_LOG2

    acc_scr[...] = jnp.zeros_like(acc_scr)
    carry_scr[...] = jnp.zeros_like(carry_scr)
    sweep_block(qi, True)

    def cond(state):
        kb, go = state
        return jnp.logical_and(kb >= 0, go)

    def body(state):
        kb, _ = state
        sweep_block(kb, False)
        return kb - 1, stick_left()

    lax.while_loop(cond, body, (qi - 1, stick_left()))
    for hd in range(SB_HEADS):
        o_ref[:, hd * SB_HEAD_DIM:(hd + 1) * SB_HEAD_DIM] = acc_scr[hd].astype(BF16)


def _attention(proj, *, bsz, seq, tq=256):
    cols = proj.shape[1]
    proj3 = proj.reshape(bsz, seq, cols)
    w = SB_WIDTH
    out = pl.pallas_call(
        functools.partial(_attn_kernel, tq=tq),
        grid=(bsz, seq // tq),
        in_specs=[
            pl.BlockSpec((None, tq, w), lambda b, i: (b, i, 0)),
            pl.BlockSpec((None, seq, w), lambda b, i: (b, 0, 1), pipeline_mode=pl.Buffered(1)),
            pl.BlockSpec((None, seq, w), lambda b, i: (b, 0, 2), pipeline_mode=pl.Buffered(1)),
        ],
        out_specs=pl.BlockSpec((None, tq, w), lambda b, i: (b, i, 0)),
        out_shape=jax.ShapeDtypeStruct((bsz, seq, w), BF16),
        scratch_shapes=[pltpu.VMEM((SB_HEADS, tq, SB_HEAD_DIM), F32),
                        pltpu.VMEM((SB_HEADS, tq, 1), F32)],
        compiler_params=pltpu.CompilerParams(
            dimension_semantics=("parallel", "arbitrary"),
            vmem_limit_bytes=_vmem_limit(2 * seq * w * 2, 4 * tq * w * 2, 8 * SB_HEADS * tq * tq * 4)),
    )(proj3, proj3, proj3)
    return out.reshape(bsz * seq, w)


def _cv_kernel(a_ref, b_ref, ah_ref, bh_ref, wdw_ref, bdw_ref, lg_ref, lb_ref, o_ref,
               pad_scr, shift_scr, *, tm, tps):
    i = pl.program_id(0)
    pad_scr[CV_HALO:, :] = a_ref[...].astype(F32) * jax.nn.sigmoid(b_ref[...].astype(F32))

    @pl.when(i % tps == 0)
    def _():
        pad_scr[:CV_HALO, :] = jnp.zeros((CV_HALO, BRANCH_WIDTH), F32)

    @pl.when(i % tps != 0)
    def _():
        pad_scr[:CV_HALO, :] = ah_ref[...].astype(F32) * jax.nn.sigmoid(bh_ref[...].astype(F32))

    span = tm + CV_HALO - V7X_SUBLANES_F32
    for r in range(1, V7X_SUBLANES_F32):
        shift_scr[r - 1] = pad_scr[r:r + span, :]

    for c in range(tm // CV_ROWS):
        base = c * CV_ROWS
        acc = jnp.broadcast_to(bdw_ref[...], (CV_ROWS, BRANCH_WIDTH))
        for tap in range(CV_FILTER):
            blk, r = divmod(CV_HALO - (CV_FILTER - 1) + tap, V7X_SUBLANES_F32)
            lo = base + blk * V7X_SUBLANES_F32
            win = pad_scr[lo:lo + CV_ROWS, :] if r == 0 else shift_scr[r - 1, lo:lo + CV_ROWS, :]
            acc = acc + wdw_ref[tap:tap + 1, :] * win
        y = _layer_norm(acc, lg_ref[...], lb_ref[...])
        o_ref[base:base + CV_ROWS, :] = (y * jax.nn.sigmoid(y)).astype(BF16)


def _conformer(proj, w_dw, b_dw, ln_g, ln_b, *, seq, tm=256):
    n = proj.shape[0]
    w = BRANCH_WIDTH
    tps = seq // tm
    col_a = 3 * SB_WIDTH // w
    col_b = col_a + 1
    hb = tm // CV_HALO
    row = lambda v: v.reshape(1, w)
    return pl.pallas_call(
        functools.partial(_cv_kernel, tm=tm, tps=tps),
        grid=(n // tm,),
        in_specs=[
            pl.BlockSpec((tm, w), lambda i: (i, col_a)),
            pl.BlockSpec((tm, w), lambda i: (i, col_b)),
            pl.BlockSpec((CV_HALO, w), lambda i: (jnp.maximum(i * hb - 1, 0), col_a)),
            pl.BlockSpec((CV_HALO, w), lambda i: (jnp.maximum(i * hb - 1, 0), col_b)),
            pl.BlockSpec((CV_FILTER, w), lambda i: (0, 0)),
            pl.BlockSpec((1, w), lambda i: (0, 0)),
            pl.BlockSpec((1, w), lambda i: (0, 0)),
            pl.BlockSpec((1, w), lambda i: (0, 0)),
        ],
        out_specs=pl.BlockSpec((tm, w), lambda i: (i, 0)),
        out_shape=jax.ShapeDtypeStruct((n, w), BF16),
        scratch_shapes=[
            pltpu.VMEM((tm + CV_HALO, w), F32),
            pltpu.VMEM((V7X_SUBLANES_F32 - 1, tm + CV_HALO - V7X_SUBLANES_F32, w), F32),
        ],
        compiler_params=pltpu.CompilerParams(
            dimension_semantics=("parallel",),
            vmem_limit_bytes=_vmem_limit(6 * tm * w * 2, 12 * (tm + CV_HALO) * w * 4)),
    )(proj, proj, proj, proj, w_dw, row(b_dw), row(ln_g), row(ln_b))


def _sg_kernel(u_ref, v_ref, lg_ref, lb_ref, ws_ref, bs_ref, o_ref, *, tm):
    t = SG_CHUNK
    u = jax.nn.gelu(u_ref[...].astype(F32))
    v = jax.nn.gelu(v_ref[...].astype(F32))
    vn = _layer_norm(v, lg_ref[...], lb_ref[...]).astype(BF16)
    pos_t = lax.broadcasted_iota(jnp.int32, (t, t), 0) // SG_CAUSAL_CHUNK
    pos_s = lax.broadcasted_iota(jnp.int32, (t, t), 1) // SG_CAUSAL_CHUNK
    chunk_causal = pos_s <= pos_t
    ws = [jnp.where(chunk_causal, ws_ref[g], 0.0).astype(BF16) for g in range(SG_GROUPS)]
    cg = BRANCH_WIDTH // SG_GROUPS
    first_group = lax.broadcasted_iota(jnp.int32, (t, V7X_LANES), 1) < cg
    for c in range(tm // t):
        rows = slice(c * t, (c + 1) * t)
        for p in range(BRANCH_WIDTH // V7X_LANES):
            cols = slice(p * V7X_LANES, (p + 1) * V7X_LANES)
            vb = vn[rows, cols]
            r0 = jnp.dot(ws[2 * p], vb, preferred_element_type=F32)
            r1 = jnp.dot(ws[2 * p + 1], vb, preferred_element_type=F32)
            mixed = jnp.where(first_group, r0, r1) + bs_ref[:, cols]
            o_ref[rows, cols] = (u[rows, cols] * mixed).astype(BF16)


def _spatial(proj, ln_g, ln_b, w_s, b_s, *, tm=512):
    n = proj.shape[0]
    w = BRANCH_WIDTH
    col_u = (3 * SB_WIDTH + 2 * w) // w
    col_v = col_u + 1
    bias = jnp.repeat(b_s.T, w // SG_GROUPS, axis=1)
    row = lambda v: v.reshape(1, w)
    return pl.pallas_call(
        functools.partial(_sg_kernel, tm=tm),
        grid=(n // tm,),
        in_specs=[
            pl.BlockSpec((tm, w), lambda i: (i, col_u)),
            pl.BlockSpec((tm, w), lambda i: (i, col_v)),
            pl.BlockSpec((1, w), lambda i: (0, 0)),
            pl.BlockSpec((1, w), lambda i: (0, 0)),
            pl.BlockSpec((SG_GROUPS, SG_CHUNK, SG_CHUNK), lambda i: (0, 0, 0)),
            pl.BlockSpec((SG_CHUNK, w), lambda i: (0, 0)),
        ],
        out_specs=pl.BlockSpec((tm, w), lambda i: (i, 0)),
        out_shape=jax.ShapeDtypeStruct((n, w), BF16),
        compiler_params=pltpu.CompilerParams(
            dimension_semantics=("parallel",),
            vmem_limit_bytes=_vmem_limit(6 * tm * w * 2, 8 * tm * w * 4)),
    )(proj, proj, row(ln_g), row(ln_b), w_s, bias)


def _merge_kernel(h_ref, ya_ref, yb_ref, yc_ref, wga_ref, wgb_ref, wgc_ref, bga_ref, bgb_ref, bgc_ref,
                  woa_ref, wob_ref, woc_ref, out_ref):
    h = h_ref[...]
    merged = None
    for y_ref, wg_ref, bg_ref, wout_ref in ((ya_ref, wga_ref, bga_ref, woa_ref),
                                            (yb_ref, wgb_ref, bgb_ref, wob_ref),
                                            (yc_ref, wgc_ref, bgc_ref, woc_ref)):
        gate = jax.nn.sigmoid(jnp.dot(h, wg_ref[...], preferred_element_type=F32) + bg_ref[...])
        term = gate * jnp.dot(y_ref[...], wout_ref[...], preferred_element_type=F32)
        merged = term if merged is None else merged + term
    out_ref[...] = merged.astype(BF16)


def _merge(h, ya, yb, yc, w_gate, b_gate, wo_a, wo_b, wo_c, *, tm=512, tn=512):
    n, d = h.shape
    w = BRANCH_WIDTH
    nj = d // tn
    bg = b_gate.reshape(1, 3 * d)
    gate_w = [pl.BlockSpec((d, tn), lambda j, i, k=k: (0, k * nj + j)) for k in range(3)]
    gate_b = [pl.BlockSpec((1, tn), lambda j, i, k=k: (0, k * nj + j)) for k in range(3)]
    branch = pl.BlockSpec((tm, w), lambda j, i: (i, 0))
    branch_w = pl.BlockSpec((w, tn), lambda j, i: (0, j))
    return pl.pallas_call(
        _merge_kernel,
        grid=(nj, n // tm),
        in_specs=[
            pl.BlockSpec((tm, d), lambda j, i: (i, 0)),
            branch, branch, branch,
            *gate_w, *gate_b,
            branch_w, branch_w, branch_w,
        ],
        out_specs=pl.BlockSpec((tm, tn), lambda j, i: (i, j)),
        out_shape=jax.ShapeDtypeStruct((n, d), BF16),
        compiler_params=pltpu.CompilerParams(
            dimension_semantics=("parallel", "parallel"),
            vmem_limit_bytes=_vmem_limit(2 * tm * d * 2, 6 * tm * w * 2, 2 * 3 * d * tn * 2,
                                         2 * 3 * w * tn * 2, 2 * tm * tn * 2, 8 * tm * tn * 4)),
    )(h, ya, yb, yc, w_gate, w_gate, w_gate, bg, bg, bg, wo_a, wo_b, wo_c)


def _residual_update(lhs_ref, x_ref, mod_ref, w_ref, gate_row, *out_refs):
    lhs = lhs_ref[...]
    chunk = 2 * MXU_COLS
    for c in range(x_ref.shape[1] // chunk):
        cols = slice(c * chunk, (c + 1) * chunk)
        y = jnp.dot(lhs, w_ref[:, cols], preferred_element_type=F32)
        xn = x_ref[:, cols] + mod_ref[gate_row:gate_row + 1, cols] * y
        for out_ref in out_refs:
            out_ref[:, cols] = xn


def _proj_residual_kernel(lhs_ref, x_ref, mod_ref, w_ref, xo_ref, *, gate_row):
    _residual_update(lhs_ref, x_ref, mod_ref, w_ref, gate_row, xo_ref)


def _proj_residual_norm_kernel(lhs_ref, x_ref, mod_ref, w_ref, g_ref, modp_ref, xo_ref, hn_ref, prev_scr,
                               *, gate_row, shift_row, scale_row):
    @pl.when(pl.program_id(0) == 0)
    def _():
        prev_scr[...] = jnp.zeros_like(prev_scr)

    hn_ref[...] = _rms_modulate(prev_scr[...], g_ref[...], modp_ref[shift_row:shift_row + 1, :],
                                modp_ref[scale_row:scale_row + 1, :]).astype(BF16)
    _residual_update(lhs_ref, x_ref, mod_ref, w_ref, gate_row, xo_ref, prev_scr)


def _proj_residual_final_kernel(lhs_ref, x_ref, mod_ref, w_ref, g_ref, yo_ref, prev_scr, *, gate_row):
    @pl.when(pl.program_id(0) == 0)
    def _():
        prev_scr[...] = jnp.zeros_like(prev_scr)

    yo_ref[...] = _rms_norm(prev_scr[...], g_ref[...])
    _residual_update(lhs_ref, x_ref, mod_ref, w_ref, gate_row, prev_scr)


def _proj_residual(lhs, x, mod, w, *, seq, gate_row, norm, g=None, shift_row=None, scale_row=None, tm=512):
    n, d = x.shape
    kdim = lhs.shape[1]
    tps = seq // tm
    nt = n // tm
    cur = lambda s: jnp.minimum(s, nt - 1)
    prev = lambda s: jnp.maximum(s - 1, 0)
    in_specs = [
        pl.BlockSpec((tm, kdim), lambda s: (cur(s), 0)),
        pl.BlockSpec((tm, d), lambda s: (cur(s), 0)),
        pl.BlockSpec((None, 6, d), lambda s: (cur(s) // tps, 0, 0)),
        pl.BlockSpec((kdim, d), lambda s: (0, 0), pipeline_mode=pl.Buffered(1)),
    ]
    args = [lhs, x, mod, w]
    row_f32 = pl.BlockSpec((tm, d), lambda s: (cur(s), 0))
    vmem = [2 * tm * kdim * 2, 4 * tm * d * 4, kdim * d * 2, tm * d * 4]
    if norm is None:
        body = functools.partial(_proj_residual_kernel, gate_row=gate_row)
        grid, out_specs, scratch = (nt,), row_f32, []
        out_shape = jax.ShapeDtypeStruct((n, d), F32)
    else:
        grid = (nt + 1,)
        scratch = [pltpu.VMEM((tm, d), F32)]
        in_specs.append(pl.BlockSpec((1, d), lambda s: (0, 0)))
        args.append(g.reshape(1, d))
        vmem += [tm * d * 4, 2 * tm * d * 4]
        if norm == "modulated":
            body = functools.partial(_proj_residual_norm_kernel, gate_row=gate_row,
                                     shift_row=shift_row, scale_row=scale_row)
            in_specs.append(pl.BlockSpec((None, 6, d), lambda s: (prev(s) // tps, 0, 0)))
            args.append(mod)
            out_specs = [row_f32, pl.BlockSpec((tm, d), lambda s: (prev(s), 0))]
            out_shape = [jax.ShapeDtypeStruct((n, d), F32), jax.ShapeDtypeStruct((n, d), BF16)]
        else:
            body = functools.partial(_proj_residual_final_kernel, gate_row=gate_row)
            out_specs = pl.BlockSpec((tm, d), lambda s: (prev(s), 0))
            out_shape = jax.ShapeDtypeStruct((n, d), F32)
    return pl.pallas_call(
        body, grid=grid, in_specs=in_specs, out_specs=out_specs, out_shape=out_shape,
        scratch_shapes=scratch,
        compiler_params=pltpu.CompilerParams(
            dimension_semantics=("arbitrary",), vmem_limit_bytes=_vmem_limit(*vmem)),
    )(*args)


def _ffn_up_kernel(h_ref, wg_ref, wv_ref, wdw_ref, bdw_ref, act_ref, ext_scr, *, tm, tps):
    i = pl.program_id(1)
    h = h_ref[...]

    @pl.when(i % tps == 0)
    def _():
        ext_scr[:FFN_HALO, :] = jnp.zeros((FFN_HALO, ext_scr.shape[1]), F32)

    for c in range(ext_scr.shape[1] // MXU_COLS):
        cols = slice(c * MXU_COLS, (c + 1) * MXU_COLS)
        gate_pre = jnp.dot(h, wg_ref[:, cols], preferred_element_type=F32)
        val = jnp.dot(h, wv_ref[:, cols], preferred_element_type=F32)
        ext_scr[FFN_HALO:, cols] = gate_pre
        conv = bdw_ref[:, cols] + wdw_ref[FFN_FILTER - 1:FFN_FILTER, cols] * gate_pre
        for tap in range(FFN_FILTER - 1):
            off = FFN_HALO - (FFN_FILTER - 1) + tap
            conv = conv + wdw_ref[tap:tap + 1, cols] * ext_scr[off:off + tm, cols]
        act_ref[:, cols] = (conv * jax.nn.sigmoid(conv) * val).astype(BF16)
        ext_scr[:FFN_HALO, cols] = gate_pre[tm - FFN_HALO:, :]


def _ffn_up(h, w_up, w_dw, b_dw, *, seq, tm=512, th=1024):
    n, d = h.shape
    f = w_up.shape[1] // 2
    tps = seq // tm
    nj = f // th
    return pl.pallas_call(
        functools.partial(_ffn_up_kernel, tm=tm, tps=tps),
        grid=(nj, n // tm),
        in_specs=[
            pl.BlockSpec((tm, d), lambda j, i: (i, 0)),
            pl.BlockSpec((d, th), lambda j, i: (0, j)),
            pl.BlockSpec((d, th), lambda j, i: (0, nj + j)),
            pl.BlockSpec((FFN_FILTER, th), lambda j, i: (0, j)),
            pl.BlockSpec((1, th), lambda j, i: (0, j)),
        ],
        out_specs=pl.BlockSpec((tm, th), lambda j, i: (i, j)),
        out_shape=jax.ShapeDtypeStruct((n, f), BF16),
        scratch_shapes=[pltpu.VMEM((tm + FFN_HALO, th), F32)],
        compiler_params=pltpu.CompilerParams(
            dimension_semantics=("arbitrary", "arbitrary"),
            vmem_limit_bytes=_vmem_limit(2 * tm * d * 2, 2 * 2 * d * th * 2, 2 * tm * th * 2,
                                         6 * tm * th * 4)),
    )(h, w_up, w_up, w_dw, b_dw.reshape(1, f))


def kernel(x, c, w_ada, b_ada, g_norm1, g_norm2, w_in, sb_w_out, cv_w_dw, cv_b_dw, cv_ln_g, cv_ln_b,
           cv_w_out, sg_ln_g, sg_ln_b, sg_w_s, sg_b_s, sg_w_out, w_gate, b_gate, w_o, ffn_w_up,
           ffn_w_dw, ffn_b_dw, ffn_w_down, g_final):
    bsz, seq, d = x.shape
    depth = w_in.shape[0]
    in_cols = w_in.shape[2]
    xf = x.reshape(bsz * seq, d)
    mod = _ada_mod(c, w_ada, b_ada)

    q_scale = math.log2(math.e) / math.sqrt(SB_HEAD_DIM)
    col_scale = jnp.where(jnp.arange(in_cols) < SB_WIDTH, q_scale, 1.0).astype(F32)

    for l in range(depth):
        w_in_l = (w_in[l] * col_scale).astype(BF16)
        proj, h = _inproj(xf, mod[l], g_norm1[l], w_in_l, seq=seq)
        y_a = _attention(proj, bsz=bsz, seq=seq)
        y_b = _conformer(proj, cv_w_dw[l], cv_b_dw[l], cv_ln_g[l], cv_ln_b[l], seq=seq)
        y_c = _spatial(proj, sg_ln_g[l], sg_ln_b[l], sg_w_s[l], sg_b_s[l])
        merged = _merge(h, y_a, y_b, y_c, w_gate[l].astype(BF16), b_gate[l], sb_w_out[l].astype(BF16),
                        cv_w_out[l].astype(BF16), sg_w_out[l].astype(BF16))
        xf, h2 = _proj_residual(merged, xf, mod[l], w_o[l].astype(BF16), seq=seq, gate_row=GATE1,
                                norm="modulated", g=g_norm2[l], shift_row=SHIFT2, scale_row=SCALE2)
        act = _ffn_up(h2, ffn_w_up[l].astype(BF16), ffn_w_dw[l], ffn_b_dw[l], seq=seq)
        last = l == depth - 1
        xf = _proj_residual(act, xf, mod[l], ffn_w_down[l].astype(BF16), seq=seq, gate_row=GATE2,
                            norm="final" if last else None, g=g_final if last else None)
    return xf.reshape(bsz, seq, d)
```

```python
import functools
import math

import jax
import jax.numpy as jnp
from jax import lax
from jax.experimental import pallas as pl
from jax.experimental.pallas import tpu as pltpu

F32 = jnp.float32
BF16 = jnp.bfloat16

EPS = 1e-6
SB_HEADS = 4
SB_HEAD_DIM = 128
SB_WIDTH = SB_HEADS * SB_HEAD_DIM
BRANCH_WIDTH = 512
CV_FILTER = 31
SG_GROUPS = 8
SG_CHUNK = 128
SG_CAUSAL_CHUNK = 64
FFN_FILTER = 3

V7X_VMEM_BYTES = 64 * 1024 * 1024
V7X_LANES = 128
V7X_SUBLANES_F32 = 8
V7X_SUBLANES_BF16 = 16
MXU_COLS = 256

SB_UNDERFLOW_LOG2 = 150.0

CV_HALO = 2 * V7X_SUBLANES_BF16
CV_ROWS = 32
FFN_HALO = V7X_SUBLANES_F32

SHIFT1, SCALE1, GATE1, SHIFT2, SCALE2, GATE2 = range(6)


def _vmem_limit(*nbytes):
    need = sum(nbytes)
    return int(min(need + need // 4, V7X_VMEM_BYTES - 4 * 1024 * 1024))


def _rms_norm(x, g):
    ms = jnp.mean(x * x, axis=-1, keepdims=True)
    return x * lax.rsqrt(ms + EPS) * g


def _rms_modulate(x, g, shift, scale):
    return _rms_norm(x, g) * (1.0 + scale) + shift


def _layer_norm(x, g, b):
    mu = jnp.mean(x, axis=-1, keepdims=True)
    xc = x - mu
    var = jnp.mean(xc * xc, axis=-1, keepdims=True)
    return xc * lax.rsqrt(var + EPS) * g + b


def _mod_kernel(c_ref, w_ref, b_ref, o_ref):
    c = c_ref[...]
    cond = (c * jax.nn.sigmoid(c)).astype(BF16)
    o_ref[...] = jnp.dot(cond, w_ref[...].astype(BF16), preferred_element_type=F32) + b_ref[...]


def _ada_mod(c, w_ada, b_ada):
    depth, d, cols = w_ada.shape
    bsz = c.shape[0]
    rows = V7X_SUBLANES_BF16
    tn = 1536
    c_pad = jnp.pad(c, ((0, rows - bsz), (0, 0)))
    out = pl.pallas_call(
        _mod_kernel,
        grid=(depth, cols // tn),
        in_specs=[
            pl.BlockSpec((rows, d), lambda l, j: (0, 0)),
            pl.BlockSpec((None, d, tn), lambda l, j: (l, 0, j)),
            pl.BlockSpec((None, 1, tn), lambda l, j: (l, 0, j)),
        ],
        out_specs=pl.BlockSpec((None, rows, tn), lambda l, j: (l, 0, j)),
        out_shape=jax.ShapeDtypeStruct((depth, rows, cols), F32),
        compiler_params=pltpu.CompilerParams(
            dimension_semantics=("parallel", "parallel"),
            vmem_limit_bytes=_vmem_limit(2 * d * tn * 4, d * tn * 2)),
    )(c_pad, w_ada, b_ada.reshape(depth, 1, cols))
    return out[:, :bsz].reshape(depth, bsz, 6, d)


def _inproj_kernel(x_ref, mod_ref, g_ref, w_ref, proj_ref, h_ref, h_even, h_odd):
    s = pl.program_id(0)

    @pl.when(s == 0)
    def _():
        h_odd[...] = jnp.zeros_like(h_odd)

    def step(h_cur, h_next):
        h = h_cur[...]
        h_ref[...] = h
        proj_ref[...] = jnp.dot(h, w_ref[...], preferred_element_type=F32).astype(BF16)
        h_next[...] = _rms_modulate(x_ref[...], g_ref[...], mod_ref[SHIFT1:SHIFT1 + 1, :],
                                    mod_ref[SCALE1:SCALE1 + 1, :]).astype(BF16)

    @pl.when(s % 2 == 0)
    def _():
        step(h_odd, h_even)

    @pl.when(s % 2 == 1)
    def _():
        step(h_even, h_odd)


def _inproj(x, mod, g, w, *, seq, tm=512):
    n, d = x.shape
    cols = w.shape[1]
    tps = seq // tm
    nt = n // tm
    cur = lambda s: jnp.minimum(s, nt - 1)
    prev = lambda s: jnp.maximum(s - 1, 0)
    return pl.pallas_call(
        _inproj_kernel,
        grid=(nt + 1,),
        in_specs=[
            pl.BlockSpec((tm, d), lambda s: (cur(s), 0)),
            pl.BlockSpec((None, 6, d), lambda s: (cur(s) // tps, 0, 0)),
            pl.BlockSpec((1, d), lambda s: (0, 0)),
            pl.BlockSpec((d, cols), lambda s: (0, 0), pipeline_mode=pl.Buffered(1)),
        ],
        out_specs=[
            pl.BlockSpec((tm, cols), lambda s: (prev(s), 0)),
            pl.BlockSpec((tm, d), lambda s: (prev(s), 0)),
        ],
        out_shape=[
            jax.ShapeDtypeStruct((n, cols), BF16),
            jax.ShapeDtypeStruct((n, d), BF16),
        ],
        scratch_shapes=[pltpu.VMEM((tm, d), BF16), pltpu.VMEM((tm, d), BF16)],
        compiler_params=pltpu.CompilerParams(
            dimension_semantics=("arbitrary",),
            vmem_limit_bytes=_vmem_limit(2 * tm * d * 4, d * cols * 2, 2 * tm * cols * 2,
                                         4 * tm * d * 2, tm * cols * 4)),
    )(x, mod, g.reshape(1, d), w)


def _attn_kernel(q_ref, k_ref, v_ref, o_ref, acc_scr, carry_scr, *, tq):
    qi = pl.program_id(1)
    row = lax.broadcasted_iota(jnp.int32, (tq, tq), 0)
    col = lax.broadcasted_iota(jnp.int32, (tq, tq), 1)
    later = jnp.where(row > col, 1.0, 0.0).astype(BF16)
    causal = col < row

    def sweep_block(kb, diagonal):
        start = pl.multiple_of(kb * tq, tq)
        for hd in range(SB_HEADS):
            cols = slice(hd * SB_HEAD_DIM, (hd + 1) * SB_HEAD_DIM)
            q = q_ref[:, cols]
            k = k_ref[pl.ds(start, tq), cols]
            v = v_ref[pl.ds(start, tq), cols]
            z = lax.dot_general(q, k, (((1,), (1,)), ((), ())), preferred_element_type=F32)
            neg_log1m = jnp.maximum(z, 0.0) + jnp.log2(1.0 + jnp.exp2(-jnp.abs(z)))
            log_beta = z - neg_log1m
            if diagonal:
                neg_log1m = jnp.where(causal, neg_log1m, 0.0)
            within = jnp.dot(neg_log1m.astype(BF16), later, preferred_element_type=F32)
            carry = carry_scr[hd]
            w = jnp.exp2(log_beta - (within + carry))
            if diagonal:
                w = jnp.where(causal, w, 0.0)
            acc_scr[hd] += jnp.dot(w.astype(BF16), v, preferred_element_type=F32)
            carry_scr[hd] = carry + jnp.sum(neg_log1m, axis=-1, keepdims=True)

    def stick_left():
        return jnp.min(carry_scr[...]) < SB_UNDERFLOW_LOG2

    acc_scr[...] = jnp.zeros_like(acc_scr)
    carry_scr[...] = jnp.zeros_like(carry_scr)
    sweep_block(qi, True)

    def cond(state):
        kb, go = state
        return jnp.logical_and(kb >= 0, go)

    def body(state):
        kb, _ = state
        sweep_block(kb, False)
        return kb - 1, stick_left()

    lax.while_loop(cond, body, (qi - 1, stick_left()))
    for hd in range(SB_HEADS):
        o_ref[:, hd * SB_HEAD_DIM:(hd + 1) * SB_HEAD_DIM] = acc_scr[hd].astype(BF16)


def _attention(proj, *, bsz, seq, tq=256):
    cols = proj.shape[1]
    proj3 = proj.reshape(bsz, seq, cols)
    w = SB_WIDTH
    out = pl.pallas_call(
        functools.partial(_attn_kernel, tq=tq),
        grid=(bsz, seq // tq),
        in_specs=[
            pl.BlockSpec((None, tq, w), lambda b, i: (b, i, 0)),
            pl.BlockSpec((None, seq, w), lambda b, i: (b, 0, 1), pipeline_mode=pl.Buffered(1)),
            pl.BlockSpec((None, seq, w), lambda b, i: (b, 0, 2), pipeline_mode=pl.Buffered(1)),
        ],
        out_specs=pl.BlockSpec((None, tq, w), lambda b, i: (b, i, 0)),
        out_shape=jax.ShapeDtypeStruct((bsz, seq, w), BF16),
        scratch_shapes=[pltpu.VMEM((SB_HEADS, tq, SB_HEAD_DIM), F32),
                        pltpu.VMEM((SB_HEADS, tq, 1), F32)],
        compiler_params=pltpu.CompilerParams(
            dimension_semantics=("parallel", "arbitrary"),
            vmem_limit_bytes=_vmem_limit(2 * seq * w * 2, 4 * tq * w * 2, 8 * SB_HEADS * tq * tq * 4)),
    )(proj3, proj3, proj3)
    return out.reshape(bsz * seq, w)


def _cv_kernel(a_ref, b_ref, ah_ref, bh_ref, wdw_ref, bdw_ref, lg_ref, lb_ref, o_ref,
               pad_scr, shift_scr, *, tm, tps):
    i = pl.program_id(0)
    pad_scr[CV_HALO:, :] = a_ref[...].astype(F32) * jax.nn.sigmoid(b_ref[...].astype(F32))

    @pl.when(i % tps == 0)
    def _():
        pad_scr[:CV_HALO, :] = jnp.zeros((CV_HALO, BRANCH_WIDTH), F32)

    @pl.when(i % tps != 0)
    def _():
        pad_scr[:CV_HALO, :] = ah_ref[...].astype(F32) * jax.nn.sigmoid(bh_ref[...].astype(F32))

    span = tm + CV_HALO - V7X_SUBLANES_F32
    for r in range(1, V7X_SUBLANES_F32):
        shift_scr[r - 1] = pad_scr[r:r + span, :]

    for c in range(tm // CV_ROWS):
        base = c * CV_ROWS
        acc = jnp.broadcast_to(bdw_ref[...], (CV_ROWS, BRANCH_WIDTH))
        for tap in range(CV_FILTER):
            blk, r = divmod(CV_HALO - (CV_FILTER - 1) + tap, V7X_SUBLANES_F32)
            lo = base + blk * V7X_SUBLANES_F32
            win = pad_scr[lo:lo + CV_ROWS, :] if r == 0 else shift_scr[r - 1, lo:lo + CV_ROWS, :]
            acc = acc + wdw_ref[tap:tap + 1, :] * win
        y = _layer_norm(acc, lg_ref[...], lb_ref[...])
        o_ref[base:base + CV_ROWS, :] = (y * jax.nn.sigmoid(y)).astype(BF16)


def _conformer(proj, w_dw, b_dw, ln_g, ln_b, *, seq, tm=256):
    n = proj.shape[0]
    w = BRANCH_WIDTH
    tps = seq // tm
    col_a = 3 * SB_WIDTH // w
    col_b = col_a + 1
    hb = tm // CV_HALO
    row = lambda v: v.reshape(1, w)
    return pl.pallas_call(
        functools.partial(_cv_kernel, tm=tm, tps=tps),
        grid=(n // tm,),
        in_specs=[
            pl.BlockSpec((tm, w), lambda i: (i, col_a)),
            pl.BlockSpec((tm, w), lambda i: (i, col_b)),
            pl.BlockSpec((CV_HALO, w), lambda i: (jnp.maximum(i * hb - 1, 0), col_a)),
            pl.BlockSpec((CV_HALO, w), lambda i: (jnp.maximum(i * hb - 1, 0), col_b)),
            pl.BlockSpec((CV_FILTER, w), lambda i: (0, 0)),
            pl.BlockSpec((1, w), lambda i: (0, 0)),
            pl.BlockSpec((1, w), lambda i: (0, 0)),
            pl.BlockSpec((1, w), lambda i: (0, 0)),
        ],
        out_specs=pl.BlockSpec((tm, w), lambda i: (i, 0)),
        out_shape=jax.ShapeDtypeStruct((n, w), BF16),
        scratch_shapes=[
            pltpu.VMEM((tm + CV_HALO, w), F32),
            pltpu.VMEM((V7X_SUBLANES_F32 - 1, tm + CV_HALO - V7X_SUBLANES_F32, w), F32),
        ],
        compiler_params=pltpu.CompilerParams(
            dimension_semantics=("parallel",),
            vmem_limit_bytes=_vmem_limit(6 * tm * w * 2, 12 * (tm + CV_HALO) * w * 4)),
    )(proj, proj, proj, proj, w_dw, row(b_dw), row(ln_g), row(ln_b))


def _sg_kernel(u_ref, v_ref, lg_ref, lb_ref, ws_ref, bs_ref, o_ref, *, tm):
    t = SG_CHUNK
    u = jax.nn.gelu(u_ref[...].astype(F32))
    v = jax.nn.gelu(v_ref[...].astype(F32))
    vn = _layer_norm(v, lg_ref[...], lb_ref[...]).astype(BF16)
    pos_t = lax.broadcasted_iota(jnp.int32, (t, t), 0) // SG_CAUSAL_CHUNK
    pos_s = lax.broadcasted_iota(jnp.int32, (t, t), 1) // SG_CAUSAL_CHUNK
    chunk_causal = pos_s <= pos_t
    ws = [jnp.where(chunk_causal, ws_ref[g], 0.0).astype(BF16) for g in range(SG_GROUPS)]
    cg = BRANCH_WIDTH // SG_GROUPS
    first_group = lax.broadcasted_iota(jnp.int32, (t, V7X_LANES), 1) < cg
    for c in range(tm // t):
        rows = slice(c * t, (c + 1) * t)
        for p in range(BRANCH_WIDTH // V7X_LANES):
            cols = slice(p * V7X_LANES, (p + 1) * V7X_LANES)
            vb = vn[rows, cols]
            r0 = jnp.dot(ws[2 * p], vb, preferred_element_type=F32)
            r1 = jnp.dot(ws[2 * p + 1], vb, preferred_element_type=F32)
            mixed = jnp.where(first_group, r0, r1) + bs_ref[:, cols]
            o_ref[rows, cols] = (u[rows, cols] * mixed).astype(BF16)


def _spatial(proj, ln_g, ln_b, w_s, b_s, *, tm=512):
    n = proj.shape[0]
    w = BRANCH_WIDTH
    col_u = (3 * SB_WIDTH + 2 * w) // w
    col_v = col_u + 1
    bias = jnp.repeat(b_s.T, w // SG_GROUPS, axis=1)
    row = lambda v: v.reshape(1, w)
    return pl.pallas_call(
        functools.partial(_sg_kernel, tm=tm),
        grid=(n // tm,),
        in_specs=[
            pl.BlockSpec((tm, w), lambda i: (i, col_u)),
            pl.BlockSpec((tm, w), lambda i: (i, col_v)),
            pl.BlockSpec((1, w), lambda i: (0, 0)),
            pl.BlockSpec((1, w), lambda i: (0, 0)),
            pl.BlockSpec((SG_GROUPS, SG_CHUNK, SG_CHUNK), lambda i: (0, 0, 0)),
            pl.BlockSpec((SG_CHUNK, w), lambda i: (0, 0)),
        ],
        out_specs=pl.BlockSpec((tm, w), lambda i: (i, 0)),
        out_shape=jax.ShapeDtypeStruct((n, w), BF16),
        compiler_params=pltpu.CompilerParams(
            dimension_semantics=("parallel",),
            vmem_limit_bytes=_vmem_limit(6 * tm * w * 2, 8 * tm * w * 4)),
    )(proj, proj, row(ln_g), row(ln_b), w_s, bias)


def _merge_kernel(h_ref, ya_ref, yb_ref, yc_ref, wga_ref, wgb_ref, wgc_ref, bga_ref, bgb_ref, bgc_ref,
                  woa_ref, wob_ref, woc_ref, out_ref, wg_scr, wo_scr):
    @pl.when(pl.program_id(1) == 0)
    def _():
        for k, (wg_ref, wout_ref) in enumerate(((wga_ref, woa_ref), (wgb_ref, wob_ref), (wgc_ref, woc_ref))):
            wg_scr[k] = wg_ref[...].astype(BF16)
            wo_scr[k] = wout_ref[...].astype(BF16)

    for c in range(out_ref.shape[1] // MXU_COLS):
        cols = slice(c * MXU_COLS, (c + 1) * MXU_COLS)
        merged = None
        for k, (y_ref, bg_ref) in enumerate(((ya_ref, bga_ref), (yb_ref, bgb_ref), (yc_ref, bgc_ref))):
            gate = jax.nn.sigmoid(
                jnp.dot(h_ref[...], wg_scr[k, :, cols], preferred_element_type=F32) + bg_ref[:, cols])
            term = gate * jnp.dot(y_ref[...], wo_scr[k, :, cols], preferred_element_type=F32)
            merged = term if merged is None else merged + term
        out_ref[:, cols] = merged.astype(BF16)


def _merge(h, ya, yb, yc, w_gate, b_gate, wo_a, wo_b, wo_c, *, layer, tm=1024, tn=512):
    n, d = h.shape
    w = BRANCH_WIDTH
    nj = d // tn
    bg = b_gate[layer].reshape(1, 3 * d)
    once = pl.Buffered(1)
    gate_w = [pl.BlockSpec((None, d, tn), lambda j, i, k=k: (layer, 0, k * nj + j), pipeline_mode=once)
              for k in range(3)]
    gate_b = [pl.BlockSpec((1, tn), lambda j, i, k=k: (0, k * nj + j)) for k in range(3)]
    branch = pl.BlockSpec((tm, w), lambda j, i: (i, 0))
    branch_w = pl.BlockSpec((None, w, tn), lambda j, i: (layer, 0, j), pipeline_mode=once)
    return pl.pallas_call(
        _merge_kernel,
        grid=(nj, n // tm),
        in_specs=[
            pl.BlockSpec((tm, d), lambda j, i: (i, 0)),
            branch, branch, branch,
            *gate_w, *gate_b,
            branch_w, branch_w, branch_w,
        ],
        out_specs=pl.BlockSpec((tm, tn), lambda j, i: (i, j)),
        out_shape=jax.ShapeDtypeStruct((n, d), BF16),
        scratch_shapes=[pltpu.VMEM((3, d, tn), BF16), pltpu.VMEM((3, w, tn), BF16)],
        compiler_params=pltpu.CompilerParams(
            dimension_semantics=("arbitrary", "arbitrary"),
            vmem_limit_bytes=_vmem_limit(2 * tm * d * 2, 6 * tm * w * 2, 3 * d * tn * (4 + 2),
                                         3 * w * tn * (4 + 2), 2 * tm * tn * 2, 8 * tm * MXU_COLS * 4)),
    )(h, ya, yb, yc, w_gate, w_gate, w_gate, bg, bg, bg, wo_a, wo_b, wo_c)


def _residual_update(lhs_ref, x_ref, mod_ref, w_ref, gate_row, *out_refs):
    chunk = 2 * MXU_COLS
    for c in range(x_ref.shape[1] // chunk):
        cols = slice(c * chunk, (c + 1) * chunk)
        y = jnp.dot(lhs_ref[...], w_ref[:, cols], preferred_element_type=F32)
        xn = x_ref[:, cols] + mod_ref[gate_row:gate_row + 1, cols] * y
        for out_ref in out_refs:
            out_ref[:, cols] = xn


def _proj_residual_kernel(lhs_ref, x_ref, mod_ref, w_ref, xo_ref, *, gate_row):
    _residual_update(lhs_ref, x_ref, mod_ref, w_ref, gate_row, xo_ref)


def _proj_residual_norm_kernel(lhs_ref, x_ref, mod_ref, w_ref, g_ref, modp_ref, xo_ref, hn_ref, prev_scr,
                               *, gate_row, shift_row, scale_row):
    @pl.when(pl.program_id(0) == 0)
    def _():
        prev_scr[...] = jnp.zeros_like(prev_scr)

    hn_ref[...] = _rms_modulate(prev_scr[...], g_ref[...], modp_ref[shift_row:shift_row + 1, :],
                                modp_ref[scale_row:scale_row + 1, :]).astype(BF16)
    _residual_update(lhs_ref, x_ref, mod_ref, w_ref, gate_row, xo_ref, prev_scr)


def _proj_residual_final_kernel(lhs_ref, x_ref, mod_ref, w_ref, g_ref, yo_ref, prev_scr, *, gate_row):
    @pl.when(pl.program_id(0) == 0)
    def _():
        prev_scr[...] = jnp.zeros_like(prev_scr)

    yo_ref[...] = _rms_norm(prev_scr[...], g_ref[...])
    _residual_update(lhs_ref, x_ref, mod_ref, w_ref, gate_row, prev_scr)


def _proj_residual(lhs, x, mod, w, *, seq, gate_row, norm, g=None, shift_row=None, scale_row=None, tm=512):
    n, d = x.shape
    kdim = lhs.shape[1]
    tps = seq // tm
    nt = n // tm
    cur = lambda s: jnp.minimum(s, nt - 1)
    prev = lambda s: jnp.maximum(s - 1, 0)
    in_specs = [
        pl.BlockSpec((tm, kdim), lambda s: (cur(s), 0)),
        pl.BlockSpec((tm, d), lambda s: (cur(s), 0)),
        pl.BlockSpec((None, 6, d), lambda s: (cur(s) // tps, 0, 0)),
        pl.BlockSpec((kdim, d), lambda s: (0, 0), pipeline_mode=pl.Buffered(1)),
    ]
    args = [lhs, x, mod, w]
    row_f32 = pl.BlockSpec((tm, d), lambda s: (cur(s), 0))
    vmem = [2 * tm * kdim * 2, 4 * tm * d * 4, kdim * d * 2, tm * d * 4]
    if norm is None:
        body = functools.partial(_proj_residual_kernel, gate_row=gate_row)
        grid, out_specs, scratch = (nt,), row_f32, []
        out_shape = jax.ShapeDtypeStruct((n, d), F32)
    else:
        grid = (nt + 1,)
        scratch = [pltpu.VMEM((tm, d), F32)]
        in_specs.append(pl.BlockSpec((1, d), lambda s: (0, 0)))
        args.append(g.reshape(1, d))
        vmem += [tm * d * 4, 2 * tm * d * 4]
        if norm == "modulated":
            body = functools.partial(_proj_residual_norm_kernel, gate_row=gate_row,
                                     shift_row=shift_row, scale_row=scale_row)
            in_specs.append(pl.BlockSpec((None, 6, d), lambda s: (prev(s) // tps, 0, 0)))
            args.append(mod)
            out_specs = [row_f32, pl.BlockSpec((tm, d), lambda s: (prev(s), 0))]
            out_shape = [jax.ShapeDtypeStruct((n, d), F32), jax.ShapeDtypeStruct((n, d), BF16)]
        else:
            body = functools.partial(_proj_residual_final_kernel, gate_row=gate_row)
            out_specs = pl.BlockSpec((tm, d), lambda s: (prev(s), 0))
            out_shape = jax.ShapeDtypeStruct((n, d), F32)
    return pl.pallas_call(
        body, grid=grid, in_specs=in_specs, out_specs=out_specs, out_shape=out_shape,
        scratch_shapes=scratch,
        compiler_params=pltpu.CompilerParams(
            dimension_semantics=("arbitrary",), vmem_limit_bytes=_vmem_limit(*vmem)),
    )(*args)


def _ffn_up_kernel(h_ref, wg_ref, wv_ref, wdw_ref, bdw_ref, act_ref, ext_scr, w_scr, *, tm, tps):
    i = pl.program_id(1)

    @pl.when(i == 0)
    def _():
        w_scr[0] = wg_ref[...].astype(BF16)
        w_scr[1] = wv_ref[...].astype(BF16)

    @pl.when(i % tps == 0)
    def _():
        ext_scr[:FFN_HALO, :] = jnp.zeros((FFN_HALO, ext_scr.shape[1]), F32)

    for c in range(ext_scr.shape[1] // MXU_COLS):
        cols = slice(c * MXU_COLS, (c + 1) * MXU_COLS)
        gate_pre = jnp.dot(h_ref[...], w_scr[0, :, cols], preferred_element_type=F32)
        val = jnp.dot(h_ref[...], w_scr[1, :, cols], preferred_element_type=F32)
        ext_scr[FFN_HALO:, cols] = gate_pre
        conv = bdw_ref[:, cols] + wdw_ref[FFN_FILTER - 1:FFN_FILTER, cols] * gate_pre
        for tap in range(FFN_FILTER - 1):
            off = FFN_HALO - (FFN_FILTER - 1) + tap
            conv = conv + wdw_ref[tap:tap + 1, cols] * ext_scr[off:off + tm, cols]
        act_ref[:, cols] = (conv * jax.nn.sigmoid(conv) * val).astype(BF16)
        ext_scr[:FFN_HALO, cols] = gate_pre[tm - FFN_HALO:, :]


def _ffn_up(h, w_up, w_dw, b_dw, *, layer, seq, tm=1024, th=1024):
    n, d = h.shape
    f = w_up.shape[2] // 2
    tps = seq // tm
    nj = f // th
    return pl.pallas_call(
        functools.partial(_ffn_up_kernel, tm=tm, tps=tps),
        grid=(nj, n // tm),
        in_specs=[
            pl.BlockSpec((tm, d), lambda j, i: (i, 0)),
            pl.BlockSpec((None, d, th), lambda j, i: (layer, 0, j), pipeline_mode=pl.Buffered(1)),
            pl.BlockSpec((None, d, th), lambda j, i: (layer, 0, nj + j), pipeline_mode=pl.Buffered(1)),
            pl.BlockSpec((FFN_FILTER, th), lambda j, i: (0, j)),
            pl.BlockSpec((1, th), lambda j, i: (0, j)),
        ],
        out_specs=pl.BlockSpec((tm, th), lambda j, i: (i, j)),
        out_shape=jax.ShapeDtypeStruct((n, f), BF16),
        scratch_shapes=[pltpu.VMEM((tm + FFN_HALO, th), F32), pltpu.VMEM((2, d, th), BF16)],
        compiler_params=pltpu.CompilerParams(
            dimension_semantics=("arbitrary", "arbitrary"),
            vmem_limit_bytes=_vmem_limit(2 * tm * d * 2, 2 * d * th * (4 + 2), 2 * tm * th * 2,
                                         tm * th * 4, 6 * tm * MXU_COLS * 4)),
    )(h, w_up, w_up, w_dw, b_dw.reshape(1, f))


def kernel(x, c, w_ada, b_ada, g_norm1, g_norm2, w_in, sb_w_out, cv_w_dw, cv_b_dw, cv_ln_g, cv_ln_b,
           cv_w_out, sg_ln_g, sg_ln_b, sg_w_s, sg_b_s, sg_w_out, w_gate, b_gate, w_o, ffn_w_up,
           ffn_w_dw, ffn_b_dw, ffn_w_down, g_final):
    bsz, seq, d = x.shape
    depth = w_in.shape[0]
    in_cols = w_in.shape[2]
    xf = x.reshape(bsz * seq, d)
    mod = _ada_mod(c, w_ada, b_ada)

    q_scale = math.log2(math.e) / math.sqrt(SB_HEAD_DIM)
    col_scale = jnp.where(jnp.arange(in_cols) < SB_WIDTH, q_scale, 1.0).astype(F32)

    for l in range(depth):
        w_in_l = (w_in[l] * col_scale).astype(BF16)
        proj, h = _inproj(xf, mod[l], g_norm1[l], w_in_l, seq=seq)
        y_a = _attention(proj, bsz=bsz, seq=seq)
        y_b = _conformer(proj, cv_w_dw[l], cv_b_dw[l], cv_ln_g[l], cv_ln_b[l], seq=seq)
        y_c = _spatial(proj, sg_ln_g[l], sg_ln_b[l], sg_w_s[l], sg_b_s[l])
        merged = _merge(h, y_a, y_b, y_c, w_gate, b_gate, sb_w_out, cv_w_out, sg_w_out, layer=l)
        xf, h2 = _proj_residual(merged, xf, mod[l], w_o[l].astype(BF16), seq=seq, gate_row=GATE1,
                                norm="modulated", g=g_norm2[l], shift_row=SHIFT2, scale_row=SCALE2)
        act = _ffn_up(h2, ffn_w_up, ffn_w_dw[l], ffn_b_dw[l], layer=l, seq=seq)
        last = l == depth - 1
        xf = _proj_residual(act, xf, mod[l], ffn_w_down[l].astype(BF16), seq=seq, gate_row=GATE2,
                            norm="final" if last else None, g=g_final if last else None)
    return xf.reshape(bsz, seq, d)
```

```python
import functools
import math

import jax
import jax.numpy as jnp
from jax import lax
from jax.experimental import pallas as pl
from jax.experimental.pallas import tpu as pltpu

F32 = jnp.float32
BF16 = jnp.bfloat16

EPS = 1e-6
SB_HEADS = 4
SB_HEAD_DIM = 128
SB_WIDTH = SB_HEADS * SB_HEAD_DIM
BRANCH_WIDTH = 512
CV_FILTER = 31
SG_GROUPS = 8
SG_CHUNK = 128
SG_CAUSAL_CHUNK = 64
FFN_FILTER = 3

V7X_VMEM_BYTES = 64 * 1024 * 1024
V7X_LANES = 128
V7X_SUBLANES_F32 = 8
V7X_SUBLANES_BF16 = 16
MXU_COLS = 256

SB_UNDERFLOW_LOG2 = 150.0

CV_HALO = 2 * V7X_SUBLANES_BF16
CV_ROWS = 32
FFN_HALO = V7X_SUBLANES_F32

SHIFT1, SCALE1, GATE1, SHIFT2, SCALE2, GATE2 = range(6)


def _vmem_limit(*nbytes):
    need = sum(nbytes)
    return int(min(need + need // 4, V7X_VMEM_BYTES - 4 * 1024 * 1024))


def _rms_norm(x, g):
    ms = jnp.mean(x * x, axis=-1, keepdims=True)
    return x * lax.rsqrt(ms + EPS) * g


def _rms_modulate(x, g, shift, scale):
    return _rms_norm(x, g) * (1.0 + scale) + shift


def _layer_norm(x, g, b):
    mu = jnp.mean(x, axis=-1, keepdims=True)
    xc = x - mu
    var = jnp.mean(xc * xc, axis=-1, keepdims=True)
    return xc * lax.rsqrt(var + EPS) * g + b


def _mod_kernel(c_ref, w_ref, b_ref, o_ref):
    c = c_ref[...]
    cond = (c * jax.nn.sigmoid(c)).astype(BF16)
    o_ref[...] = jnp.dot(cond, w_ref[...].astype(BF16), preferred_element_type=F32) + b_ref[...]


def _ada_mod(c, w_ada, b_ada):
    depth, d, cols = w_ada.shape
    bsz = c.shape[0]
    rows = V7X_SUBLANES_BF16
    tn = 1536
    c_pad = jnp.pad(c, ((0, rows - bsz), (0, 0)))
    out = pl.pallas_call(
        _mod_kernel,
        grid=(depth, cols // tn),
        in_specs=[
            pl.BlockSpec((rows, d), lambda l, j: (0, 0)),
            pl.BlockSpec((None, d, tn), lambda l, j: (l, 0, j)),
            pl.BlockSpec((None, 1, tn), lambda l, j: (l, 0, j)),
        ],
        out_specs=pl.BlockSpec((None, rows, tn), lambda l, j: (l, 0, j)),
        out_shape=jax.ShapeDtypeStruct((depth, rows, cols), F32),
        compiler_params=pltpu.CompilerParams(
            dimension_semantics=("parallel", "parallel"),
            vmem_limit_bytes=_vmem_limit(2 * d * tn * 4, d * tn * 2)),
    )(c_pad, w_ada, b_ada.reshape(depth, 1, cols))
    return out[:, :bsz].reshape(depth, bsz, 6, d)


def _inproj_kernel(x_ref, mod_ref, g_ref, w_ref, proj_ref, h_ref, h_even, h_odd):
    s = pl.program_id(0)

    @pl.when(s == 0)
    def _():
        h_odd[...] = jnp.zeros_like(h_odd)

    def step(h_cur, h_next):
        h = h_cur[...]
        h_ref[...] = h
        proj_ref[...] = jnp.dot(h, w_ref[...], preferred_element_type=F32).astype(BF16)
        h_next[...] = _rms_modulate(x_ref[...], g_ref[...], mod_ref[SHIFT1:SHIFT1 + 1, :],
                                    mod_ref[SCALE1:SCALE1 + 1, :]).astype(BF16)

    @pl.when(s % 2 == 0)
    def _():
        step(h_odd, h_even)

    @pl.when(s % 2 == 1)
    def _():
        step(h_even, h_odd)


def _inproj(x, mod, g, w, *, seq, tm=512):
    n, d = x.shape
    cols = w.shape[1]
    tps = seq // tm
    nt = n // tm
    cur = lambda s: jnp.minimum(s, nt - 1)
    prev = lambda s: jnp.maximum(s - 1, 0)
    return pl.pallas_call(
        _inproj_kernel,
        grid=(nt + 1,),
        in_specs=[
            pl.BlockSpec((tm, d), lambda s: (cur(s), 0)),
            pl.BlockSpec((None, 6, d), lambda s: (cur(s) // tps, 0, 0)),
            pl.BlockSpec((1, d), lambda s: (0, 0)),
            pl.BlockSpec((d, cols), lambda s: (0, 0), pipeline_mode=pl.Buffered(1)),
        ],
        out_specs=[
            pl.BlockSpec((tm, cols), lambda s: (prev(s), 0)),
            pl.BlockSpec((tm, d), lambda s: (prev(s), 0)),
        ],
        out_shape=[
            jax.ShapeDtypeStruct((n, cols), BF16),
            jax.ShapeDtypeStruct((n, d), BF16),
        ],
        scratch_shapes=[pltpu.VMEM((tm, d), BF16), pltpu.VMEM((tm, d), BF16)],
        compiler_params=pltpu.CompilerParams(
            dimension_semantics=("arbitrary",),
            vmem_limit_bytes=_vmem_limit(2 * tm * d * 4, d * cols * 2, 2 * tm * cols * 2,
                                         4 * tm * d * 2, tm * cols * 4)),
    )(x, mod, g.reshape(1, d), w)


def _attn_kernel(q_ref, k_ref, v_ref, o_ref, acc_scr, carry_scr, *, tq):
    qi = pl.program_id(1)
    row = lax.broadcasted_iota(jnp.int32, (tq, tq), 0)
    col = lax.broadcasted_iota(jnp.int32, (tq, tq), 1)
    later = jnp.where(row > col, 1.0, 0.0).astype(BF16)
    causal = col < row

    def sweep(blocks, first):
        for hd in range(SB_HEADS):
            cols = slice(hd * SB_HEAD_DIM, (hd + 1) * SB_HEAD_DIM)
            q = q_ref[:, cols]
            carry = None if first else carry_scr[hd]
            acc = None if first else acc_scr[hd]
            for kb, diagonal in blocks:
                start = pl.multiple_of(kb * tq, tq)
                k = k_ref[pl.ds(start, tq), cols]
                v = v_ref[pl.ds(start, tq), cols]
                z = lax.dot_general(q, k, (((1,), (1,)), ((), ())), preferred_element_type=F32)
                neg_log1m = jnp.maximum(z, 0.0) + jnp.log2(1.0 + jnp.exp2(-jnp.abs(z)))
                log_beta = z - neg_log1m
                if diagonal:
                    neg_log1m = jnp.where(causal, neg_log1m, 0.0)
                suffix = jnp.dot(neg_log1m.astype(BF16), later, preferred_element_type=F32)
                if carry is not None:
                    suffix = suffix + carry
                w = jnp.exp2(log_beta - suffix)
                if diagonal:
                    w = jnp.where(causal, w, 0.0)
                pv = jnp.dot(w.astype(BF16), v, preferred_element_type=F32)
                acc = pv if acc is None else acc + pv
                block_sum = jnp.sum(neg_log1m, axis=-1, keepdims=True)
                carry = block_sum if carry is None else carry + block_sum
            acc_scr[hd] = acc
            carry_scr[hd] = carry

    def stick_left():
        return jnp.min(carry_scr[...]) < SB_UNDERFLOW_LOG2

    @pl.when(qi == 0)
    def _():
        sweep([(qi, True)], first=True)

    @pl.when(qi > 0)
    def _():
        sweep([(qi, True), (qi - 1, False)], first=True)

    def cond(state):
        kb, go = state
        return jnp.logical_and(kb >= 0, go)

    def body(state):
        kb, _ = state
        sweep([(kb, False)], first=False)
        return kb - 1, stick_left()

    lax.while_loop(cond, body, (qi - 2, stick_left()))
    for hd in range(SB_HEADS):
        o_ref[:, hd * SB_HEAD_DIM:(hd + 1) * SB_HEAD_DIM] = acc_scr[hd].astype(BF16)


def _attention(proj, *, bsz, seq, tq=256):
    cols = proj.shape[1]
    proj3 = proj.reshape(bsz, seq, cols)
    w = SB_WIDTH
    out = pl.pallas_call(
        functools.partial(_attn_kernel, tq=tq),
        grid=(bsz, seq // tq),
        in_specs=[
            pl.BlockSpec((None, tq, w), lambda b, i: (b, i, 0)),
            pl.BlockSpec((None, seq, w), lambda b, i: (b, 0, 1), pipeline_mode=pl.Buffered(1)),
            pl.BlockSpec((None, seq, w), lambda b, i: (b, 0, 2), pipeline_mode=pl.Buffered(1)),
        ],
        out_specs=pl.BlockSpec((None, tq, w), lambda b, i: (b, i, 0)),
        out_shape=jax.ShapeDtypeStruct((bsz, seq, w), BF16),
        scratch_shapes=[pltpu.VMEM((SB_HEADS, tq, SB_HEAD_DIM), F32),
                        pltpu.VMEM((SB_HEADS, tq, 1), F32)],
        compiler_params=pltpu.CompilerParams(
            dimension_semantics=("parallel", "arbitrary"),
            vmem_limit_bytes=_vmem_limit(2 * seq * w * 2, 4 * tq * w * 2, 8 * SB_HEADS * tq * tq * 4)),
    )(proj3, proj3, proj3)
    return out.reshape(bsz * seq, w)


def _cv_kernel(a_ref, b_ref, ah_ref, bh_ref, wdw_ref, bdw_ref, lg_ref, lb_ref, o_ref,
               pad_scr, shift_scr, *, tm, tps):
    i = pl.program_id(0)
    pad_scr[CV_HALO:, :] = a_ref[...].astype(F32) * jax.nn.sigmoid(b_ref[...].astype(F32))

    @pl.when(i % tps == 0)
    def _():
        pad_scr[:CV_HALO, :] = jnp.zeros((CV_HALO, BRANCH_WIDTH), F32)

    @pl.when(i % tps != 0)
    def _():
        pad_scr[:CV_HALO, :] = ah_ref[...].astype(F32) * jax.nn.sigmoid(bh_ref[...].astype(F32))

    span = tm + CV_HALO - V7X_SUBLANES_F32
    for r in range(1, V7X_SUBLANES_F32):
        shift_scr[r - 1] = pad_scr[r:r + span, :]

    groups = (CV_ROWS // V7X_SUBLANES_F32, V7X_SUBLANES_F32, BRANCH_WIDTH)
    for c in range(tm // CV_ROWS):
        base = c * CV_ROWS
        acc = jnp.broadcast_to(bdw_ref[...], groups)
        for tap in range(CV_FILTER):
            blk, r = divmod(CV_HALO - (CV_FILTER - 1) + tap, V7X_SUBLANES_F32)
            lo = base + blk * V7X_SUBLANES_F32
            win = pad_scr[lo:lo + CV_ROWS, :] if r == 0 else shift_scr[r - 1, lo:lo + CV_ROWS, :]
            acc = acc + wdw_ref[tap] * win.reshape(groups)
        y = _layer_norm(acc.reshape(CV_ROWS, BRANCH_WIDTH), lg_ref[...], lb_ref[...])
        o_ref[base:base + CV_ROWS, :] = (y * jax.nn.sigmoid(y)).astype(BF16)


def _conformer(proj, w_dw, b_dw, ln_g, ln_b, *, seq, tm=256):
    n = proj.shape[0]
    w = BRANCH_WIDTH
    tps = seq // tm
    col_a = 3 * SB_WIDTH // w
    col_b = col_a + 1
    hb = tm // CV_HALO
    row = lambda v: v.reshape(1, w)
    return pl.pallas_call(
        functools.partial(_cv_kernel, tm=tm, tps=tps),
        grid=(n // tm,),
        in_specs=[
            pl.BlockSpec((tm, w), lambda i: (i, col_a)),
            pl.BlockSpec((tm, w), lambda i: (i, col_b)),
            pl.BlockSpec((CV_HALO, w), lambda i: (jnp.maximum(i * hb - 1, 0), col_a)),
            pl.BlockSpec((CV_HALO, w), lambda i: (jnp.maximum(i * hb - 1, 0), col_b)),
            pl.BlockSpec((CV_FILTER, V7X_SUBLANES_F32, w), lambda i: (0, 0, 0)),
            pl.BlockSpec((1, w), lambda i: (0, 0)),
            pl.BlockSpec((1, w), lambda i: (0, 0)),
            pl.BlockSpec((1, w), lambda i: (0, 0)),
        ],
        out_specs=pl.BlockSpec((tm, w), lambda i: (i, 0)),
        out_shape=jax.ShapeDtypeStruct((n, w), BF16),
        scratch_shapes=[
            pltpu.VMEM((tm + CV_HALO, w), F32),
            pltpu.VMEM((V7X_SUBLANES_F32 - 1, tm + CV_HALO - V7X_SUBLANES_F32, w), F32),
        ],
        compiler_params=pltpu.CompilerParams(
            dimension_semantics=("parallel",),
            vmem_limit_bytes=_vmem_limit(6 * tm * w * 2, 12 * (tm + CV_HALO) * w * 4)),
    )(proj, proj, proj, proj, jnp.broadcast_to(w_dw[:, None, :], (CV_FILTER, V7X_SUBLANES_F32, w)),
      row(b_dw), row(ln_g), row(ln_b))


def _sg_kernel(u_ref, v_ref, lg_ref, lb_ref, ws_ref, bs_ref, o_ref, *, tm):
    t = SG_CHUNK
    u = jax.nn.gelu(u_ref[...].astype(F32))
    v = jax.nn.gelu(v_ref[...].astype(F32))
    vn = _layer_norm(v, lg_ref[...], lb_ref[...]).astype(BF16)
    pos_t = lax.broadcasted_iota(jnp.int32, (t, t), 0) // SG_CAUSAL_CHUNK
    pos_s = lax.broadcasted_iota(jnp.int32, (t, t), 1) // SG_CAUSAL_CHUNK
    chunk_causal = pos_s <= pos_t
    ws = [jnp.where(chunk_causal, ws_ref[g], 0.0).astype(BF16) for g in range(SG_GROUPS)]
    cg = BRANCH_WIDTH // SG_GROUPS
    first_group = lax.broadcasted_iota(jnp.int32, (t, V7X_LANES), 1) < cg
    for c in range(tm // t):
        rows = slice(c * t, (c + 1) * t)
        for p in range(BRANCH_WIDTH // V7X_LANES):
            cols = slice(p * V7X_LANES, (p + 1) * V7X_LANES)
            vb = vn[rows, cols]
            r0 = jnp.dot(ws[2 * p], vb, preferred_element_type=F32)
            r1 = jnp.dot(ws[2 * p + 1], vb, preferred_element_type=F32)
            mixed = jnp.where(first_group, r0, r1) + bs_ref[:, cols]
            o_ref[rows, cols] = (u[rows, cols] * mixed).astype(BF16)


def _spatial(proj, ln_g, ln_b, w_s, b_s, *, tm=512):
    n = proj.shape[0]
    w = BRANCH_WIDTH
    col_u = (3 * SB_WIDTH + 2 * w) // w
    col_v = col_u + 1
    bias = jnp.repeat(b_s.T, w // SG_GROUPS, axis=1)
    row = lambda v: v.reshape(1, w)
    return pl.pallas_call(
        functools.partial(_sg_kernel, tm=tm),
        grid=(n // tm,),
        in_specs=[
            pl.BlockSpec((tm, w), lambda i: (i, col_u)),
            pl.BlockSpec((tm, w), lambda i: (i, col_v)),
            pl.BlockSpec((1, w), lambda i: (0, 0)),
            pl.BlockSpec((1, w), lambda i: (0, 0)),
            pl.BlockSpec((SG_GROUPS, SG_CHUNK, SG_CHUNK), lambda i: (0, 0, 0)),
            pl.BlockSpec((SG_CHUNK, w), lambda i: (0, 0)),
        ],
        out_specs=pl.BlockSpec((tm, w), lambda i: (i, 0)),
        out_shape=jax.ShapeDtypeStruct((n, w), BF16),
        compiler_params=pltpu.CompilerParams(
            dimension_semantics=("parallel",),
            vmem_limit_bytes=_vmem_limit(6 * tm * w * 2, 8 * tm * w * 4)),
    )(proj, proj, row(ln_g), row(ln_b), w_s, bias)


def _merge_kernel(h_ref, ya_ref, yb_ref, yc_ref, wga_ref, wgb_ref, wgc_ref, bga_ref, bgb_ref, bgc_ref,
                  woa_ref, wob_ref, woc_ref, out_ref, wg_scr, wo_scr):
    @pl.when(pl.program_id(1) == 0)
    def _():
        for k, (wg_ref, wout_ref) in enumerate(((wga_ref, woa_ref), (wgb_ref, wob_ref), (wgc_ref, woc_ref))):
            wg_scr[k] = wg_ref[...].astype(BF16)
            wo_scr[k] = wout_ref[...].astype(BF16)

    for c in range(out_ref.shape[1] // MXU_COLS):
        cols = slice(c * MXU_COLS, (c + 1) * MXU_COLS)
        merged = None
        for k, (y_ref, bg_ref) in enumerate(((ya_ref, bga_ref), (yb_ref, bgb_ref), (yc_ref, bgc_ref))):
            gate = jax.nn.sigmoid(
                jnp.dot(h_ref[...], wg_scr[k, :, cols], preferred_element_type=F32) + bg_ref[:, cols])
            term = gate * jnp.dot(y_ref[...], wo_scr[k, :, cols], preferred_element_type=F32)
            merged = term if merged is None else merged + term
        out_ref[:, cols] = merged.astype(BF16)


def _merge(h, ya, yb, yc, w_gate, b_gate, wo_a, wo_b, wo_c, *, layer, tm=1024, tn=512):
    n, d = h.shape
    w = BRANCH_WIDTH
    nj = d // tn
    bg = b_gate[layer].reshape(1, 3 * d)
    once = pl.Buffered(1)
    gate_w = [pl.BlockSpec((None, d, tn), lambda j, i, k=k: (layer, 0, k * nj + j), pipeline_mode=once)
              for k in range(3)]
    gate_b = [pl.BlockSpec((1, tn), lambda j, i, k=k: (0, k * nj + j)) for k in range(3)]
    branch = pl.BlockSpec((tm, w), lambda j, i: (i, 0))
    branch_w = pl.BlockSpec((None, w, tn), lambda j, i: (layer, 0, j), pipeline_mode=once)
    return pl.pallas_call(
        _merge_kernel,
        grid=(nj, n // tm),
        in_specs=[
            pl.BlockSpec((tm, d), lambda j, i: (i, 0)),
            branch, branch, branch,
            *gate_w, *gate_b,
            branch_w, branch_w, branch_w,
        ],
        out_specs=pl.BlockSpec((tm, tn), lambda j, i: (i, j)),
        out_shape=jax.ShapeDtypeStruct((n, d), BF16),
        scratch_shapes=[pltpu.VMEM((3, d, tn), BF16), pltpu.VMEM((3, w, tn), BF16)],
        compiler_params=pltpu.CompilerParams(
            dimension_semantics=("arbitrary", "arbitrary"),
            vmem_limit_bytes=_vmem_limit(2 * tm * d * 2, 6 * tm * w * 2, 3 * d * tn * (4 + 2),
                                         3 * w * tn * (4 + 2), 2 * tm * tn * 2, 8 * tm * MXU_COLS * 4)),
    )(h, ya, yb, yc, w_gate, w_gate, w_gate, bg, bg, bg, wo_a, wo_b, wo_c)


def _residual_update(lhs_ref, x_ref, mod_ref, w_ref, gate_row, *out_refs):
    chunk = 2 * MXU_COLS
    for c in range(x_ref.shape[1] // chunk):
        cols = slice(c * chunk, (c + 1) * chunk)
        y = jnp.dot(lhs_ref[...], w_ref[:, cols], preferred_element_type=F32)
        xn = x_ref[:, cols] + mod_ref[gate_row:gate_row + 1, cols] * y
        for out_ref in out_refs:
            out_ref[:, cols] = xn


def _proj_residual_kernel(lhs_ref, x_ref, mod_ref, w_ref, xo_ref, *, gate_row):
    _residual_update(lhs_ref, x_ref, mod_ref, w_ref, gate_row, xo_ref)


def _alternate(step, even_scr, odd_scr):
    s = pl.program_id(0)

    @pl.when(s == 0)
    def _():
        odd_scr[...] = jnp.zeros_like(odd_scr)

    @pl.when(s % 2 == 0)
    def _():
        step(odd_scr, even_scr)

    @pl.when(s % 2 == 1)
    def _():
        step(even_scr, odd_scr)


def _proj_residual_norm_kernel(lhs_ref, x_ref, mod_ref, w_ref, g_ref, modp_ref, xo_ref, hn_ref,
                               even_scr, odd_scr, *, gate_row, shift_row, scale_row):
    def step(prev_scr, cur_scr):
        hn_ref[...] = _rms_modulate(prev_scr[...], g_ref[...], modp_ref[shift_row:shift_row + 1, :],
                                    modp_ref[scale_row:scale_row + 1, :]).astype(BF16)
        _residual_update(lhs_ref, x_ref, mod_ref, w_ref, gate_row, xo_ref, cur_scr)

    _alternate(step, even_scr, odd_scr)


def _proj_residual_final_kernel(lhs_ref, x_ref, mod_ref, w_ref, g_ref, yo_ref, even_scr, odd_scr, *, gate_row):
    def step(prev_scr, cur_scr):
        yo_ref[...] = _rms_norm(prev_scr[...], g_ref[...])
        _residual_update(lhs_ref, x_ref, mod_ref, w_ref, gate_row, cur_scr)

    _alternate(step, even_scr, odd_scr)


def _proj_residual(lhs, x, mod, w, *, seq, gate_row, norm, g=None, shift_row=None, scale_row=None, tm=512):
    n, d = x.shape
    kdim = lhs.shape[1]
    tps = seq // tm
    nt = n // tm
    cur = lambda s: jnp.minimum(s, nt - 1)
    prev = lambda s: jnp.maximum(s - 1, 0)
    in_specs = [
        pl.BlockSpec((tm, kdim), lambda s: (cur(s), 0)),
        pl.BlockSpec((tm, d), lambda s: (cur(s), 0)),
        pl.BlockSpec((None, 6, d), lambda s: (cur(s) // tps, 0, 0)),
        pl.BlockSpec((kdim, d), lambda s: (0, 0), pipeline_mode=pl.Buffered(1)),
    ]
    args = [lhs, x, mod, w]
    row_f32 = pl.BlockSpec((tm, d), lambda s: (cur(s), 0))
    vmem = [2 * tm * kdim * 2, 4 * tm * d * 4, kdim * d * 2, tm * d * 4]
    if norm is None:
        body = functools.partial(_proj_residual_kernel, gate_row=gate_row)
        grid, out_specs, scratch = (nt,), row_f32, []
        out_shape = jax.ShapeDtypeStruct((n, d), F32)
    else:
        grid = (nt + 1,)
        scratch = [pltpu.VMEM((tm, d), F32), pltpu.VMEM((tm, d), F32)]
        in_specs.append(pl.BlockSpec((1, d), lambda s: (0, 0)))
        args.append(g.reshape(1, d))
        vmem += [2 * tm * d * 4, 2 * tm * d * 4]
        if norm == "modulated":
            body = functools.partial(_proj_residual_norm_kernel, gate_row=gate_row,
                                     shift_row=shift_row, scale_row=scale_row)
            in_specs.append(pl.BlockSpec((None, 6, d), lambda s: (prev(s) // tps, 0, 0)))
            args.append(mod)
            out_specs = [row_f32, pl.BlockSpec((tm, d), lambda s: (prev(s), 0))]
            out_shape = [jax.ShapeDtypeStruct((n, d), F32), jax.ShapeDtypeStruct((n, d), BF16)]
        else:
            body = functools.partial(_proj_residual_final_kernel, gate_row=gate_row)
            out_specs = pl.BlockSpec((tm, d), lambda s: (prev(s), 0))
            out_shape = jax.ShapeDtypeStruct((n, d), F32)
    return pl.pallas_call(
        body, grid=grid, in_specs=in_specs, out_specs=out_specs, out_shape=out_shape,
        scratch_shapes=scratch,
        compiler_params=pltpu.CompilerParams(
            dimension_semantics=("arbitrary",), vmem_limit_bytes=_vmem_limit(*vmem)),
    )(*args)


def _ffn_up_kernel(h_ref, wg_ref, wv_ref, wdw_ref, bdw_ref, act_ref, ext_scr, w_scr, *, tm, tps):
    i = pl.program_id(1)

    @pl.when(i == 0)
    def _():
        w_scr[0] = wg_ref[...].astype(BF16)
        w_scr[1] = wv_ref[...].astype(BF16)

    @pl.when(i % tps == 0)
    def _():
        ext_scr[...] = jnp.zeros_like(ext_scr)

    for c in range(ext_scr.shape[1] // MXU_COLS):
        cols = slice(c * MXU_COLS, (c + 1) * MXU_COLS)
        gate_pre = jnp.dot(h_ref[...], w_scr[0, :, cols], preferred_element_type=F32)
        val = jnp.dot(h_ref[...], w_scr[1, :, cols], preferred_element_type=F32)
        history = jnp.concatenate([ext_scr[:, cols], gate_pre], axis=0)
        conv = bdw_ref[:, cols] + wdw_ref[FFN_FILTER - 1:FFN_FILTER, cols] * gate_pre
        for tap in range(FFN_FILTER - 1):
            back = FFN_FILTER - 1 - tap
            conv = conv + wdw_ref[tap:tap + 1, cols] * pltpu.roll(history, back, axis=0)[FFN_HALO:, :]
        act_ref[:, cols] = (conv * jax.nn.sigmoid(conv) * val).astype(BF16)
        ext_scr[:, cols] = gate_pre[tm - FFN_HALO:, :]


def _ffn_up(h, w_up, w_dw, b_dw, *, layer, seq, tm=1024, th=1024):
    n, d = h.shape
    f = w_up.shape[2] // 2
    tps = seq // tm
    nj = f // th
    return pl.pallas_call(
        functools.partial(_ffn_up_kernel, tm=tm, tps=tps),
        grid=(nj, n // tm),
        in_specs=[
            pl.BlockSpec((tm, d), lambda j, i: (i, 0)),
            pl.BlockSpec((None, d, th), lambda j, i: (layer, 0, j), pipeline_mode=pl.Buffered(1)),
            pl.BlockSpec((None, d, th), lambda j, i: (layer, 0, nj + j), pipeline_mode=pl.Buffered(1)),
            pl.BlockSpec((FFN_FILTER, th), lambda j, i: (0, j)),
            pl.BlockSpec((1, th), lambda j, i: (0, j)),
        ],
        out_specs=pl.BlockSpec((tm, th), lambda j, i: (i, j)),
        out_shape=jax.ShapeDtypeStruct((n, f), BF16),
        scratch_shapes=[pltpu.VMEM((FFN_HALO, th), F32), pltpu.VMEM((2, d, th), BF16)],
        compiler_params=pltpu.CompilerParams(
            dimension_semantics=("arbitrary", "arbitrary"),
            vmem_limit_bytes=_vmem_limit(2 * tm * d * 2, 2 * d * th * (4 + 2), 2 * tm * th * 2,
                                         tm * th * 4, 6 * tm * MXU_COLS * 4)),
    )(h, w_up, w_up, w_dw, b_dw.reshape(1, f))


def kernel(x, c, w_ada, b_ada, g_norm1, g_norm2, w_in, sb_w_out, cv_w_dw, cv_b_dw, cv_ln_g, cv_ln_b,
           cv_w_out, sg_ln_g, sg_ln_b, sg_w_s, sg_b_s, sg_w_out, w_gate, b_gate, w_o, ffn_w_up,
           ffn_w_dw, ffn_b_dw, ffn_w_down, g_final):
    bsz, seq, d = x.shape
    depth = w_in.shape[0]
    in_cols = w_in.shape[2]
    xf = x.reshape(bsz * seq, d)
    mod = _ada_mod(c, w_ada, b_ada)

    q_scale = math.log2(math.e) / math.sqrt(SB_HEAD_DIM)
    col_scale = jnp.where(jnp.arange(in_cols) < SB_WIDTH, q_scale, 1.0).astype(F32)

    for l in range(depth):
        w_in_l = (w_in[l] * col_scale).astype(BF16)
        proj, h = _inproj(xf, mod[l], g_norm1[l], w_in_l, seq=seq)
        y_a = _attention(proj, bsz=bsz, seq=seq)
        y_b = _conformer(proj, cv_w_dw[l], cv_b_dw[l], cv_ln_g[l], cv_ln_b[l], seq=seq)
        y_c = _spatial(proj, sg_ln_g[l], sg_ln_b[l], sg_w_s[l], sg_b_s[l])
        merged = _merge(h, y_a, y_b, y_c, w_gate, b_gate, sb_w_out, cv_w_out, sg_w_out, layer=l)
        xf, h2 = _proj_residual(merged, xf, mod[l], w_o[l].astype(BF16), seq=seq, gate_row=GATE1,
                                norm="modulated", g=g_norm2[l], shift_row=SHIFT2, scale_row=SCALE2)
        act = _ffn_up(h2, ffn_w_up, ffn_w_dw[l], ffn_b_dw[l], layer=l, seq=seq)
        last = l == depth - 1
        xf = _proj_residual(act, xf, mod[l], ffn_w_down[l].astype(BF16), seq=seq, gate_row=GATE2,
                            norm="final" if last else None, g=g_final if last else None)
    return xf.reshape(bsz, seq, d)
```

```python
import functools
import math

import jax
import jax.numpy as jnp
from jax import lax
from jax.experimental import pallas as pl
from jax.experimental.pallas import tpu as pltpu

F32 = jnp.float32
BF16 = jnp.bfloat16

EPS = 1e-6
SB_HEADS = 4
SB_HEAD_DIM = 128
SB_WIDTH = SB_HEADS * SB_HEAD_DIM
BRANCH_WIDTH = 512
CV_FILTER = 31
SG_GROUPS = 8
SG_CHUNK = 128
SG_CAUSAL_CHUNK = 64
FFN_FILTER = 3

V7X_VMEM_BYTES = 64 * 1024 * 1024
V7X_LANES = 128
V7X_SUBLANES_F32 = 8
V7X_SUBLANES_BF16 = 16
MXU_COLS = 256

SB_UNDERFLOW_LOG2 = 150.0

CV_HALO = 2 * V7X_SUBLANES_BF16
CV_STRIDE = 4
FFN_HALO = V7X_SUBLANES_F32

SHIFT1, SCALE1, GATE1, SHIFT2, SCALE2, GATE2 = range(6)


def _vmem_limit(*nbytes):
    need = sum(nbytes)
    return int(min(need + need // 4, V7X_VMEM_BYTES - 4 * 1024 * 1024))


def _rms_norm(x, g):
    ms = jnp.mean(x * x, axis=-1, keepdims=True)
    return x * lax.rsqrt(ms + EPS) * g


def _rms_modulate(x, g, shift, scale):
    return _rms_norm(x, g * (1.0 + scale)) + shift


def _layer_norm(x, g, b):
    mu = jnp.mean(x, axis=-1, keepdims=True)
    xc = x - mu
    var = jnp.mean(xc * xc, axis=-1, keepdims=True)
    return xc * lax.rsqrt(var + EPS) * g + b


def _mod_kernel(c_ref, w_ref, b_ref, o_ref):
    c = c_ref[...]
    cond = (c * jax.nn.sigmoid(c)).astype(BF16)
    o_ref[...] = jnp.dot(cond, w_ref[...].astype(BF16), preferred_element_type=F32) + b_ref[...]


def _ada_mod(c, w_ada, b_ada):
    depth, d, cols = w_ada.shape
    bsz = c.shape[0]
    rows = V7X_SUBLANES_BF16
    tn = 1536
    c_pad = jnp.pad(c, ((0, rows - bsz), (0, 0)))
    out = pl.pallas_call(
        _mod_kernel,
        grid=(depth, cols // tn),
        in_specs=[
            pl.BlockSpec((rows, d), lambda l, j: (0, 0)),
            pl.BlockSpec((None, d, tn), lambda l, j: (l, 0, j)),
            pl.BlockSpec((None, 1, tn), lambda l, j: (l, 0, j)),
        ],
        out_specs=pl.BlockSpec((None, rows, tn), lambda l, j: (l, 0, j)),
        out_shape=jax.ShapeDtypeStruct((depth, rows, cols), F32),
        compiler_params=pltpu.CompilerParams(
            dimension_semantics=("parallel", "parallel"),
            vmem_limit_bytes=_vmem_limit(2 * d * tn * 4, d * tn * 2)),
    )(c_pad, w_ada, b_ada.reshape(depth, 1, cols))
    return out[:, :bsz].reshape(depth, bsz, 6, d)


def _inproj_kernel(x_ref, mod_ref, g_ref, w_ref, proj_ref, h_ref, h_even, h_odd):
    s = pl.program_id(0)

    @pl.when(s == 0)
    def _():
        h_odd[...] = jnp.zeros_like(h_odd)

    def step(h_cur, h_next):
        h = h_cur[...]
        h_ref[...] = h
        proj_ref[...] = jnp.dot(h, w_ref[...], preferred_element_type=F32).astype(BF16)
        h_next[...] = _rms_modulate(x_ref[...], g_ref[...], mod_ref[SHIFT1:SHIFT1 + 1, :],
                                    mod_ref[SCALE1:SCALE1 + 1, :]).astype(BF16)

    @pl.when(s % 2 == 0)
    def _():
        step(h_odd, h_even)

    @pl.when(s % 2 == 1)
    def _():
        step(h_even, h_odd)


def _inproj(x, mod, g, w, *, seq, tm=512):
    n, d = x.shape
    cols = w.shape[1]
    tps = seq // tm
    nt = n // tm
    cur = lambda s: jnp.minimum(s, nt - 1)
    prev = lambda s: jnp.maximum(s - 1, 0)
    return pl.pallas_call(
        _inproj_kernel,
        grid=(nt + 1,),
        in_specs=[
            pl.BlockSpec((tm, d), lambda s: (cur(s), 0)),
            pl.BlockSpec((None, 6, d), lambda s: (cur(s) // tps, 0, 0)),
            pl.BlockSpec((1, d), lambda s: (0, 0)),
            pl.BlockSpec((d, cols), lambda s: (0, 0), pipeline_mode=pl.Buffered(1)),
        ],
        out_specs=[
            pl.BlockSpec((tm, cols), lambda s: (prev(s), 0)),
            pl.BlockSpec((tm, d), lambda s: (prev(s), 0)),
        ],
        out_shape=[
            jax.ShapeDtypeStruct((n, cols), BF16),
            jax.ShapeDtypeStruct((n, d), BF16),
        ],
        scratch_shapes=[pltpu.VMEM((tm, d), BF16), pltpu.VMEM((tm, d), BF16)],
        compiler_params=pltpu.CompilerParams(
            dimension_semantics=("arbitrary",),
            vmem_limit_bytes=_vmem_limit(2 * tm * d * 4, d * cols * 2, 2 * tm * cols * 2,
                                         4 * tm * d * 2, tm * cols * 4)),
    )(x, mod, g.reshape(1, d), w)


def _attn_kernel(q_ref, k_ref, v_ref, o_ref, acc_scr, carry_scr, *, tq):
    qi = pl.program_id(1)
    row = lax.broadcasted_iota(jnp.int32, (tq, tq), 0)
    col = lax.broadcasted_iota(jnp.int32, (tq, tq), 1)
    later = jnp.where(row > col, 1.0, 0.0).astype(BF16)
    causal = col < row

    def sweep(blocks, first):
        for hd in range(SB_HEADS):
            cols = slice(hd * SB_HEAD_DIM, (hd + 1) * SB_HEAD_DIM)
            q = q_ref[:, cols]
            carry = None if first else carry_scr[hd]
            acc = None if first else acc_scr[hd]
            for kb, diagonal in blocks:
                start = pl.multiple_of(kb * tq, tq)
                k = k_ref[pl.ds(start, tq), cols]
                v = v_ref[pl.ds(start, tq), cols]
                z = lax.dot_general(q, k, (((1,), (1,)), ((), ())), preferred_element_type=F32)
                neg_log1m = jnp.maximum(z, 0.0) + jnp.log2(1.0 + jnp.exp2(-jnp.abs(z)))
                log_beta = z - neg_log1m
                if diagonal:
                    neg_log1m = jnp.where(causal, neg_log1m, 0.0)
                suffix = jnp.dot(neg_log1m.astype(BF16), later, preferred_element_type=F32)
                if carry is not None:
                    suffix = suffix + carry
                w = jnp.exp2(log_beta - suffix)
                if diagonal:
                    w = jnp.where(causal, w, 0.0)
                pv = jnp.dot(w.astype(BF16), v, preferred_element_type=F32)
                acc = pv if acc is None else acc + pv
                block_sum = jnp.sum(neg_log1m, axis=-1, keepdims=True)
                carry = block_sum if carry is None else carry + block_sum
            acc_scr[hd] = acc
            carry_scr[hd] = carry

    def stick_left():
        return jnp.min(carry_scr[...]) < SB_UNDERFLOW_LOG2

    @pl.when(qi == 0)
    def _():
        sweep([(qi, True)], first=True)

    @pl.when(qi > 0)
    def _():
        sweep([(qi, True), (qi - 1, False)], first=True)

    def cond(state):
        kb, go = state
        return jnp.logical_and(kb >= 0, go)

    def body(state):
        kb, _ = state
        sweep([(kb, False)], first=False)
        return kb - 1, stick_left()

    lax.while_loop(cond, body, (qi - 2, stick_left()))
    for hd in range(SB_HEADS):
        o_ref[:, hd * SB_HEAD_DIM:(hd + 1) * SB_HEAD_DIM] = acc_scr[hd].astype(BF16)


def _attention(proj, *, bsz, seq, tq=256):
    cols = proj.shape[1]
    proj3 = proj.reshape(bsz, seq, cols)
    w = SB_WIDTH
    out = pl.pallas_call(
        functools.partial(_attn_kernel, tq=tq),
        grid=(bsz, seq // tq),
        in_specs=[
            pl.BlockSpec((None, tq, w), lambda b, i: (b, i, 0)),
            pl.BlockSpec((None, seq, w), lambda b, i: (b, 0, 1), pipeline_mode=pl.Buffered(1)),
            pl.BlockSpec((None, seq, w), lambda b, i: (b, 0, 2), pipeline_mode=pl.Buffered(1)),
        ],
        out_specs=pl.BlockSpec((None, tq, w), lambda b, i: (b, i, 0)),
        out_shape=jax.ShapeDtypeStruct((bsz, seq, w), BF16),
        scratch_shapes=[pltpu.VMEM((SB_HEADS, tq, SB_HEAD_DIM), F32),
                        pltpu.VMEM((SB_HEADS, tq, 1), F32)],
        compiler_params=pltpu.CompilerParams(
            dimension_semantics=("parallel", "arbitrary"),
            vmem_limit_bytes=_vmem_limit(2 * seq * w * 2, 4 * tq * w * 2, 8 * SB_HEADS * tq * tq * 4)),
    )(proj3, proj3, proj3)
    return out.reshape(bsz * seq, w)


def _cv_kernel(a_ref, b_ref, ah_ref, bh_ref, wdw_ref, bdw_ref, lg_ref, lb_ref, o_ref,
               pad_scr, out_scr, *, tm, tps):
    i = pl.program_id(0)
    slabs = BRANCH_WIDTH // V7X_LANES
    lanes = lambda k: slice(k * V7X_LANES, (k + 1) * V7X_LANES)
    tile_rows = lambda first: pl.ds(first, V7X_SUBLANES_F32, stride=CV_STRIDE)

    glu = a_ref[...].astype(F32) * jax.nn.sigmoid(b_ref[...].astype(F32))
    for k in range(slabs):
        pad_scr[k, CV_HALO:, :] = glu[:, lanes(k)]

    @pl.when(i % tps == 0)
    def _():
        pad_scr[:, :CV_HALO, :] = jnp.zeros((slabs, CV_HALO, V7X_LANES), F32)

    @pl.when(i % tps != 0)
    def _():
        halo = ah_ref[...].astype(F32) * jax.nn.sigmoid(bh_ref[...].astype(F32))
        for k in range(slabs):
            pad_scr[k, :CV_HALO, :] = halo[:, lanes(k)]

    group_rows = CV_STRIDE * V7X_SUBLANES_F32

    def conv_group(g, _):
        for k in range(slabs):
            acc = [bdw_ref[:, lanes(k)]] * CV_STRIDE
            for tap in range(CV_FILTER):
                w_tap = wdw_ref[tap, :, lanes(k)]
                for s in range(CV_STRIDE):
                    src = g * group_rows + (s + CV_HALO - (CV_FILTER - 1) + tap)
                    acc[s] = acc[s] + w_tap * pad_scr[k, tile_rows(src), :]
            for s in range(CV_STRIDE):
                out_scr[k, tile_rows(g * group_rows + s), :] = acc[s]
        return _

    lax.fori_loop(0, tm // group_rows, conv_group, 0)
    y = _layer_norm(jnp.concatenate([out_scr[k] for k in range(slabs)], axis=1), lg_ref[...], lb_ref[...])
    o_ref[...] = (y * jax.nn.sigmoid(y)).astype(BF16)


def _conformer(proj, w_dw, b_dw, ln_g, ln_b, *, seq, tm=512):
    n = proj.shape[0]
    w = BRANCH_WIDTH
    tps = seq // tm
    col_a = 3 * SB_WIDTH // w
    col_b = col_a + 1
    hb = tm // CV_HALO
    row = lambda v: v.reshape(1, w)
    return pl.pallas_call(
        functools.partial(_cv_kernel, tm=tm, tps=tps),
        grid=(n // tm,),
        in_specs=[
            pl.BlockSpec((tm, w), lambda i: (i, col_a)),
            pl.BlockSpec((tm, w), lambda i: (i, col_b)),
            pl.BlockSpec((CV_HALO, w), lambda i: (jnp.maximum(i * hb - 1, 0), col_a)),
            pl.BlockSpec((CV_HALO, w), lambda i: (jnp.maximum(i * hb - 1, 0), col_b)),
            pl.BlockSpec((CV_FILTER, V7X_SUBLANES_F32, w), lambda i: (0, 0, 0)),
            pl.BlockSpec((V7X_SUBLANES_F32, w), lambda i: (0, 0)),
            pl.BlockSpec((1, w), lambda i: (0, 0)),
            pl.BlockSpec((1, w), lambda i: (0, 0)),
        ],
        out_specs=pl.BlockSpec((tm, w), lambda i: (i, 0)),
        out_shape=jax.ShapeDtypeStruct((n, w), BF16),
        scratch_shapes=[
            pltpu.VMEM((w // V7X_LANES, tm + CV_HALO, V7X_LANES), F32),
            pltpu.VMEM((w // V7X_LANES, tm, V7X_LANES), F32),
        ],
        compiler_params=pltpu.CompilerParams(
            dimension_semantics=("parallel",),
            vmem_limit_bytes=_vmem_limit(6 * tm * w * 2, 6 * (tm + CV_HALO) * w * 4)),
    )(proj, proj, proj, proj, jnp.broadcast_to(w_dw[:, None, :], (CV_FILTER, V7X_SUBLANES_F32, w)),
      jnp.broadcast_to(b_dw[None, :], (V7X_SUBLANES_F32, w)), row(ln_g), row(ln_b))


def _sg_kernel(u_ref, v_ref, lg_ref, lb_ref, ws_ref, bs_ref, o_ref, *, tm):
    t = SG_CHUNK
    u = jax.nn.gelu(u_ref[...].astype(F32))
    v = jax.nn.gelu(v_ref[...].astype(F32))
    vn = _layer_norm(v, lg_ref[...], lb_ref[...]).astype(BF16)
    pos_t = lax.broadcasted_iota(jnp.int32, (t, t), 0) // SG_CAUSAL_CHUNK
    pos_s = lax.broadcasted_iota(jnp.int32, (t, t), 1) // SG_CAUSAL_CHUNK
    chunk_causal = pos_s <= pos_t
    ws = [jnp.where(chunk_causal, ws_ref[g], 0.0).astype(BF16) for g in range(SG_GROUPS)]
    cg = BRANCH_WIDTH // SG_GROUPS
    first_group = lax.broadcasted_iota(jnp.int32, (t, V7X_LANES), 1) < cg
    for c in range(tm // t):
        rows = slice(c * t, (c + 1) * t)
        for p in range(BRANCH_WIDTH // V7X_LANES):
            cols = slice(p * V7X_LANES, (p + 1) * V7X_LANES)
            vb = vn[rows, cols]
            r0 = jnp.dot(ws[2 * p], vb, preferred_element_type=F32)
            r1 = jnp.dot(ws[2 * p + 1], vb, preferred_element_type=F32)
            mixed = jnp.where(first_group, r0, r1) + bs_ref[:, cols]
            o_ref[rows, cols] = (u[rows, cols] * mixed).astype(BF16)


def _spatial(proj, ln_g, ln_b, w_s, b_s, *, tm=512):
    n = proj.shape[0]
    w = BRANCH_WIDTH
    col_u = (3 * SB_WIDTH + 2 * w) // w
    col_v = col_u + 1
    bias = jnp.repeat(b_s.T, w // SG_GROUPS, axis=1)
    row = lambda v: v.reshape(1, w)
    return pl.pallas_call(
        functools.partial(_sg_kernel, tm=tm),
        grid=(n // tm,),
        in_specs=[
            pl.BlockSpec((tm, w), lambda i: (i, col_u)),
            pl.BlockSpec((tm, w), lambda i: (i, col_v)),
            pl.BlockSpec((1, w), lambda i: (0, 0)),
            pl.BlockSpec((1, w), lambda i: (0, 0)),
            pl.BlockSpec((SG_GROUPS, SG_CHUNK, SG_CHUNK), lambda i: (0, 0, 0)),
            pl.BlockSpec((SG_CHUNK, w), lambda i: (0, 0)),
        ],
        out_specs=pl.BlockSpec((tm, w), lambda i: (i, 0)),
        out_shape=jax.ShapeDtypeStruct((n, w), BF16),
        compiler_params=pltpu.CompilerParams(
            dimension_semantics=("parallel",),
            vmem_limit_bytes=_vmem_limit(6 * tm * w * 2, 8 * tm * w * 4)),
    )(proj, proj, row(ln_g), row(ln_b), w_s, bias)


def _merge_kernel(h_ref, ya_ref, yb_ref, yc_ref, wga_ref, wgb_ref, wgc_ref, bga_ref, bgb_ref, bgc_ref,
                  woa_ref, wob_ref, woc_ref, out_ref, wg_scr, wo_scr):
    @pl.when(pl.program_id(1) == 0)
    def _():
        for k, (wg_ref, wout_ref) in enumerate(((wga_ref, woa_ref), (wgb_ref, wob_ref), (wgc_ref, woc_ref))):
            wg_scr[k] = wg_ref[...].astype(BF16)
            wo_scr[k] = wout_ref[...].astype(BF16)

    for c in range(out_ref.shape[1] // MXU_COLS):
        cols = slice(c * MXU_COLS, (c + 1) * MXU_COLS)
        merged = None
        for k, (y_ref, bg_ref) in enumerate(((ya_ref, bga_ref), (yb_ref, bgb_ref), (yc_ref, bgc_ref))):
            gate = jax.nn.sigmoid(
                jnp.dot(h_ref[...], wg_scr[k, :, cols], preferred_element_type=F32) + bg_ref[:, cols])
            term = gate * jnp.dot(y_ref[...], wo_scr[k, :, cols], preferred_element_type=F32)
            merged = term if merged is None else merged + term
        out_ref[:, cols] = merged.astype(BF16)


def _merge(h, ya, yb, yc, w_gate, b_gate, wo_a, wo_b, wo_c, *, layer, tm=1024, tn=512):
    n, d = h.shape
    w = BRANCH_WIDTH
    nj = d // tn
    bg = b_gate[layer].reshape(1, 3 * d)
    once = pl.Buffered(1)
    gate_w = [pl.BlockSpec((None, d, tn), lambda j, i, k=k: (layer, 0, k * nj + j), pipeline_mode=once)
              for k in range(3)]
    gate_b = [pl.BlockSpec((1, tn), lambda j, i, k=k: (0, k * nj + j)) for k in range(3)]
    branch = pl.BlockSpec((tm, w), lambda j, i: (i, 0))
    branch_w = pl.BlockSpec((None, w, tn), lambda j, i: (layer, 0, j), pipeline_mode=once)
    return pl.pallas_call(
        _merge_kernel,
        grid=(nj, n // tm),
        in_specs=[
            pl.BlockSpec((tm, d), lambda j, i: (i, 0)),
            branch, branch, branch,
            *gate_w, *gate_b,
            branch_w, branch_w, branch_w,
        ],
        out_specs=pl.BlockSpec((tm, tn), lambda j, i: (i, j)),
        out_shape=jax.ShapeDtypeStruct((n, d), BF16),
        scratch_shapes=[pltpu.VMEM((3, d, tn), BF16), pltpu.VMEM((3, w, tn), BF16)],
        compiler_params=pltpu.CompilerParams(
            dimension_semantics=("arbitrary", "arbitrary"),
            vmem_limit_bytes=_vmem_limit(2 * tm * d * 2, 6 * tm * w * 2, 3 * d * tn * (4 + 2),
                                         3 * w * tn * (4 + 2), 2 * tm * tn * 2, 8 * tm * MXU_COLS * 4)),
    )(h, ya, yb, yc, w_gate, w_gate, w_gate, bg, bg, bg, wo_a, wo_b, wo_c)


def _residual_update(lhs_ref, x_ref, mod_ref, w_ref, gate_row, *out_refs):
    chunk = 2 * MXU_COLS
    for c in range(x_ref.shape[1] // chunk):
        cols = slice(c * chunk, (c + 1) * chunk)
        y = jnp.dot(lhs_ref[...], w_ref[:, cols], preferred_element_type=F32)
        xn = x_ref[:, cols] + mod_ref[gate_row:gate_row + 1, cols] * y
        for out_ref in out_refs:
            out_ref[:, cols] = xn


def _proj_residual_kernel(lhs_ref, x_ref, mod_ref, w_ref, xo_ref, *, gate_row):
    _residual_update(lhs_ref, x_ref, mod_ref, w_ref, gate_row, xo_ref)


def _alternate(step, even_scr, odd_scr):
    s = pl.program_id(0)

    @pl.when(s == 0)
    def _():
        odd_scr[...] = jnp.zeros_like(odd_scr)

    @pl.when(s % 2 == 0)
    def _():
        step(odd_scr, even_scr)

    @pl.when(s % 2 == 1)
    def _():
        step(even_scr, odd_scr)


def _proj_residual_norm_kernel(lhs_ref, x_ref, mod_ref, w_ref, g_ref, modp_ref, xo_ref, hn_ref,
                               even_scr, odd_scr, *, gate_row, shift_row, scale_row):
    def step(prev_scr, cur_scr):
        hn_ref[...] = _rms_modulate(prev_scr[...], g_ref[...], modp_ref[shift_row:shift_row + 1, :],
                                    modp_ref[scale_row:scale_row + 1, :]).astype(BF16)
        _residual_update(lhs_ref, x_ref, mod_ref, w_ref, gate_row, xo_ref, cur_scr)

    _alternate(step, even_scr, odd_scr)


def _proj_residual_final_kernel(lhs_ref, x_ref, mod_ref, w_ref, g_ref, yo_ref, even_scr, odd_scr, *, gate_row):
    def step(prev_scr, cur_scr):
        yo_ref[...] = _rms_norm(prev_scr[...], g_ref[...])
        _residual_update(lhs_ref, x_ref, mod_ref, w_ref, gate_row, cur_scr)

    _alternate(step, even_scr, odd_scr)


def _proj_residual(lhs, x, mod, w, *, seq, gate_row, norm, g=None, shift_row=None, scale_row=None, tm=512):
    n, d = x.shape
    kdim = lhs.shape[1]
    tps = seq // tm
    nt = n // tm
    cur = lambda s: jnp.minimum(s, nt - 1)
    prev = lambda s: jnp.maximum(s - 1, 0)
    in_specs = [
        pl.BlockSpec((tm, kdim), lambda s: (cur(s), 0)),
        pl.BlockSpec((tm, d), lambda s: (cur(s), 0)),
        pl.BlockSpec((None, 6, d), lambda s: (cur(s) // tps, 0, 0)),
        pl.BlockSpec((kdim, d), lambda s: (0, 0), pipeline_mode=pl.Buffered(1)),
    ]
    args = [lhs, x, mod, w]
    row_f32 = pl.BlockSpec((tm, d), lambda s: (cur(s), 0))
    vmem = [2 * tm * kdim * 2, 4 * tm * d * 4, kdim * d * 2, tm * d * 4]
    if norm is None:
        body = functools.partial(_proj_residual_kernel, gate_row=gate_row)
        grid, out_specs, scratch = (nt,), row_f32, []
        out_shape = jax.ShapeDtypeStruct((n, d), F32)
    else:
        grid = (nt + 1,)
        scratch = [pltpu.VMEM((tm, d), F32), pltpu.VMEM((tm, d), F32)]
        in_specs.append(pl.BlockSpec((1, d), lambda s: (0, 0)))
        args.append(g.reshape(1, d))
        vmem += [2 * tm * d * 4, 2 * tm * d * 4]
        if norm == "modulated":
            body = functools.partial(_proj_residual_norm_kernel, gate_row=gate_row,
                                     shift_row=shift_row, scale_row=scale_row)
            in_specs.append(pl.BlockSpec((None, 6, d), lambda s: (prev(s) // tps, 0, 0)))
            args.append(mod)
            out_specs = [row_f32, pl.BlockSpec((tm, d), lambda s: (prev(s), 0))]
            out_shape = [jax.ShapeDtypeStruct((n, d), F32), jax.ShapeDtypeStruct((n, d), BF16)]
        else:
            body = functools.partial(_proj_residual_final_kernel, gate_row=gate_row)
            out_specs = pl.BlockSpec((tm, d), lambda s: (prev(s), 0))
            out_shape = jax.ShapeDtypeStruct((n, d), F32)
    return pl.pallas_call(
        body, grid=grid, in_specs=in_specs, out_specs=out_specs, out_shape=out_shape,
        scratch_shapes=scratch,
        compiler_params=pltpu.CompilerParams(
            dimension_semantics=("arbitrary",), vmem_limit_bytes=_vmem_limit(*vmem)),
    )(*args)


def _ffn_up_kernel(h_ref, wg_ref, wv_ref, wdw_ref, bdw_ref, act_ref, gate_scr, val_scr, out_scr, w_scr,
                   *, tm, tps):
    i = pl.program_id(1)
    lanes = lambda k: slice(k * V7X_LANES, (k + 1) * V7X_LANES)
    tile_rows = lambda first: pl.ds(first, V7X_SUBLANES_F32, stride=CV_STRIDE)
    chunk_slabs = MXU_COLS // V7X_LANES

    @pl.when(i == 0)
    def _():
        w_scr[0] = wg_ref[...].astype(BF16)
        w_scr[1] = wv_ref[...].astype(BF16)

    @pl.when(i % tps == 0)
    def _():
        gate_scr[:, :FFN_HALO, :] = jnp.zeros((gate_scr.shape[0], FFN_HALO, V7X_LANES), F32)

    for c in range(gate_scr.shape[0] // chunk_slabs):
        cols = slice(c * MXU_COLS, (c + 1) * MXU_COLS)
        gate_pre = jnp.dot(h_ref[...], w_scr[0, :, cols], preferred_element_type=F32)
        val = jnp.dot(h_ref[...], w_scr[1, :, cols], preferred_element_type=F32)
        for k in range(c * chunk_slabs, (c + 1) * chunk_slabs):
            part = lanes(k - c * chunk_slabs)
            gate_scr[k, FFN_HALO:, :] = gate_pre[:, part]
            val_scr[k] = val[:, part]
            taps = [wdw_ref[tap, :, lanes(k)] for tap in range(FFN_FILTER)]
            bias = bdw_ref[:, lanes(k)]
            for first in [g * CV_STRIDE * V7X_SUBLANES_F32 + s
                          for g in range(tm // (CV_STRIDE * V7X_SUBLANES_F32)) for s in range(CV_STRIDE)]:
                conv = bias
                for tap in range(FFN_FILTER):
                    src = first + FFN_HALO - (FFN_FILTER - 1) + tap
                    conv = conv + taps[tap] * gate_scr[k, tile_rows(src), :]
                out_scr[k, tile_rows(first), :] = conv * jax.nn.sigmoid(conv) * val_scr[k, tile_rows(first), :]
            gate_scr[k, :FFN_HALO, :] = gate_scr[k, tm:tm + FFN_HALO, :]
        act_ref[:, cols] = jnp.concatenate(
            [out_scr[k] for k in range(c * chunk_slabs, (c + 1) * chunk_slabs)], axis=1).astype(BF16)


def _ffn_up(h, w_up, w_dw, b_dw, *, layer, seq, tm=1024, th=1024):
    n, d = h.shape
    f = w_up.shape[2] // 2
    tps = seq // tm
    nj = f // th
    return pl.pallas_call(
        functools.partial(_ffn_up_kernel, tm=tm, tps=tps),
        grid=(nj, n // tm),
        in_specs=[
            pl.BlockSpec((tm, d), lambda j, i: (i, 0)),
            pl.BlockSpec((None, d, th), lambda j, i: (layer, 0, j), pipeline_mode=pl.Buffered(1)),
            pl.BlockSpec((None, d, th), lambda j, i: (layer, 0, nj + j), pipeline_mode=pl.Buffered(1)),
            pl.BlockSpec((FFN_FILTER, V7X_SUBLANES_F32, th), lambda j, i: (0, 0, j)),
            pl.BlockSpec((V7X_SUBLANES_F32, th), lambda j, i: (0, j)),
        ],
        out_specs=pl.BlockSpec((tm, th), lambda j, i: (i, j)),
        out_shape=jax.ShapeDtypeStruct((n, f), BF16),
        scratch_shapes=[pltpu.VMEM((th // V7X_LANES, tm + FFN_HALO, V7X_LANES), F32),
                        pltpu.VMEM((th // V7X_LANES, tm, V7X_LANES), F32),
                        pltpu.VMEM((th // V7X_LANES, tm, V7X_LANES), F32),
                        pltpu.VMEM((2, d, th), BF16)],
        compiler_params=pltpu.CompilerParams(
            dimension_semantics=("arbitrary", "arbitrary"),
            vmem_limit_bytes=_vmem_limit(2 * tm * d * 2, 2 * d * th * (4 + 2), 2 * tm * th * 2,
                                         3 * (tm + FFN_HALO) * th * 4, 4 * tm * MXU_COLS * 4)),
    )(h, w_up, w_up, jnp.broadcast_to(w_dw[:, None, :], (FFN_FILTER, V7X_SUBLANES_F32, f)),
      jnp.broadcast_to(b_dw[None, :], (V7X_SUBLANES_F32, f)))


def kernel(x, c, w_ada, b_ada, g_norm1, g_norm2, w_in, sb_w_out, cv_w_dw, cv_b_dw, cv_ln_g, cv_ln_b,
           cv_w_out, sg_ln_g, sg_ln_b, sg_w_s, sg_b_s, sg_w_out, w_gate, b_gate, w_o, ffn_w_up,
           ffn_w_dw, ffn_b_dw, ffn_w_down, g_final):
    bsz, seq, d = x.shape
    depth = w_in.shape[0]
    in_cols = w_in.shape[2]
    xf = x.reshape(bsz * seq, d)
    mod = _ada_mod(c, w_ada, b_ada)

    q_scale = math.log2(math.e) / math.sqrt(SB_HEAD_DIM)
    col_scale = jnp.where(jnp.arange(in_cols) < SB_WIDTH, q_scale, 1.0).astype(F32)

    for l in range(depth):
        w_in_l = (w_in[l] * col_scale).astype(BF16)
        proj, h = _inproj(xf, mod[l], g_norm1[l], w_in_l, seq=seq)
        y_a = _attention(proj, bsz=bsz, seq=seq)
        y_b = _conformer(proj, cv_w_dw[l], cv_b_dw[l], cv_ln_g[l], cv_ln_b[l], seq=seq)
        y_c = _spatial(proj, sg_ln_g[l], sg_ln_b[l], sg_w_s[l], sg_b_s[l])
        merged = _merge(h, y_a, y_b, y_c, w_gate, b_gate, sb_w_out, cv_w_out, sg_w_out, layer=l)
        xf, h2 = _proj_residual(merged, xf, mod[l], w_o[l].astype(BF16), seq=seq, gate_row=GATE1,
                                norm="modulated", g=g_norm2[l], shift_row=SHIFT2, scale_row=SCALE2)
        act = _ffn_up(h2, ffn_w_up, ffn_w_dw[l], ffn_b_dw[l], layer=l, seq=seq)
        last = l == depth - 1
        xf = _proj_residual(act, xf, mod[l], ffn_w_down[l].astype(BF16), seq=seq, gate_row=GATE2,
                            norm="final" if last else None, g=g_final if last else None)
    return xf.reshape(bsz, seq, d)
```

```python
import functools
import math

import jax
import jax.numpy as jnp
from jax import lax
from jax.experimental import pallas as pl
from jax.experimental.pallas import tpu as pltpu

F32 = jnp.float32
BF16 = jnp.bfloat16

EPS = 1e-6
SB_HEADS = 4
SB_HEAD_DIM = 128
SB_WIDTH = SB_HEADS * SB_HEAD_DIM
BRANCH_WIDTH = 512
CV_FILTER = 31
SG_GROUPS = 8
SG_CHUNK = 128
SG_CAUSAL_CHUNK = 64
FFN_FILTER = 3

V7X_VMEM_BYTES = 64 * 1024 * 1024
V7X_LANES = 128
V7X_SUBLANES_F32 = 8
V7X_SUBLANES_BF16 = 16
MXU_COLS = 256

SB_UNDERFLOW_LOG2 = 150.0

CV_HALO = 2 * V7X_SUBLANES_BF16
CV_STRIDE = 4
FFN_HALO = V7X_SUBLANES_F32

SHIFT1, SCALE1, GATE1, SHIFT2, SCALE2, GATE2 = range(6)


def _vmem_limit(*nbytes):
    need = sum(nbytes)
    return int(min(need + need // 4, V7X_VMEM_BYTES - 4 * 1024 * 1024))


def _rms_norm(x, g):
    ms = jnp.mean(x * x, axis=-1, keepdims=True)
    return x * lax.rsqrt(ms + EPS) * g


def _rms_modulate(x, g, shift, scale):
    return _rms_norm(x, g * (1.0 + scale)) + shift


def _layer_norm(x, g, b):
    mu = jnp.mean(x, axis=-1, keepdims=True)
    xc = x - mu
    var = jnp.mean(xc * xc, axis=-1, keepdims=True)
    return xc * lax.rsqrt(var + EPS) * g + b


def _mod_kernel(c_ref, w_ref, b_ref, o_ref):
    c = c_ref[...]
    cond = (c * jax.nn.sigmoid(c)).astype(BF16)
    o_ref[...] = jnp.dot(cond, w_ref[...].astype(BF16), preferred_element_type=F32) + b_ref[...]


def _ada_mod(c, w_ada, b_ada):
    depth, d, cols = w_ada.shape
    bsz = c.shape[0]
    rows = V7X_SUBLANES_BF16
    tn = 1536
    c_pad = jnp.pad(c, ((0, rows - bsz), (0, 0)))
    out = pl.pallas_call(
        _mod_kernel,
        grid=(depth, cols // tn),
        in_specs=[
            pl.BlockSpec((rows, d), lambda l, j: (0, 0)),
            pl.BlockSpec((None, d, tn), lambda l, j: (l, 0, j)),
            pl.BlockSpec((None, 1, tn), lambda l, j: (l, 0, j)),
        ],
        out_specs=pl.BlockSpec((None, rows, tn), lambda l, j: (l, 0, j)),
        out_shape=jax.ShapeDtypeStruct((depth, rows, cols), F32),
        compiler_params=pltpu.CompilerParams(
            dimension_semantics=("parallel", "parallel"),
            vmem_limit_bytes=_vmem_limit(2 * d * tn * 4, d * tn * 2)),
    )(c_pad, w_ada, b_ada.reshape(depth, 1, cols))
    return out[:, :bsz].reshape(depth, bsz, 6, d)


def _inproj_kernel(x_ref, mod_ref, g_ref, w_ref, proj_ref, h_ref, h_even, h_odd):
    s = pl.program_id(0)

    @pl.when(s == 0)
    def _():
        h_odd[...] = jnp.zeros_like(h_odd)

    def step(h_cur, h_next):
        h = h_cur[...]
        h_ref[...] = h
        proj_ref[...] = jnp.dot(h, w_ref[...], preferred_element_type=F32).astype(BF16)
        h_next[...] = _rms_modulate(x_ref[...], g_ref[...], mod_ref[SHIFT1:SHIFT1 + 1, :],
                                    mod_ref[SCALE1:SCALE1 + 1, :]).astype(BF16)

    @pl.when(s % 2 == 0)
    def _():
        step(h_odd, h_even)

    @pl.when(s % 2 == 1)
    def _():
        step(h_even, h_odd)


def _inproj(x, mod, g, w, *, layer, seq, tm=512):
    n, d = x.shape
    cols = w.shape[2]
    tps = seq // tm
    nt = n // tm
    cur = lambda s: jnp.minimum(s, nt - 1)
    prev = lambda s: jnp.maximum(s - 1, 0)
    return pl.pallas_call(
        _inproj_kernel,
        grid=(nt + 1,),
        in_specs=[
            pl.BlockSpec((tm, d), lambda s: (cur(s), 0)),
            pl.BlockSpec((None, 6, d), lambda s: (cur(s) // tps, 0, 0)),
            pl.BlockSpec((1, d), lambda s: (0, 0)),
            pl.BlockSpec((None, d, cols), lambda s: (layer, 0, 0), pipeline_mode=pl.Buffered(1)),
        ],
        out_specs=[
            pl.BlockSpec((tm, cols), lambda s: (prev(s), 0)),
            pl.BlockSpec((tm, d), lambda s: (prev(s), 0)),
        ],
        out_shape=[
            jax.ShapeDtypeStruct((n, cols), BF16),
            jax.ShapeDtypeStruct((n, d), BF16),
        ],
        scratch_shapes=[pltpu.VMEM((tm, d), BF16), pltpu.VMEM((tm, d), BF16)],
        compiler_params=pltpu.CompilerParams(
            dimension_semantics=("arbitrary",),
            vmem_limit_bytes=_vmem_limit(2 * tm * d * 4, d * cols * 2, 2 * tm * cols * 2,
                                         4 * tm * d * 2, tm * cols * 4)),
    )(x, mod, g.reshape(1, d), w)


def _attn_kernel(q_ref, k_ref, v_ref, o_ref, acc_scr, carry_scr, *, tq):
    qi = pl.program_id(1)
    row = lax.broadcasted_iota(jnp.int32, (tq, tq), 0)
    col = lax.broadcasted_iota(jnp.int32, (tq, tq), 1)
    later = jnp.where(row > col, 1.0, 0.0).astype(BF16)
    causal = col < row

    def sweep(blocks, first):
        for hd in range(SB_HEADS):
            cols = slice(hd * SB_HEAD_DIM, (hd + 1) * SB_HEAD_DIM)
            q = q_ref[:, cols]
            carry = None if first else carry_scr[hd]
            acc = None if first else acc_scr[hd]
            for kb, diagonal in blocks:
                start = pl.multiple_of(kb * tq, tq)
                k = k_ref[pl.ds(start, tq), cols]
                v = v_ref[pl.ds(start, tq), cols]
                z = lax.dot_general(q, k, (((1,), (1,)), ((), ())), preferred_element_type=F32)
                neg_log1m = jnp.maximum(z, 0.0) + jnp.log2(1.0 + jnp.exp2(-jnp.abs(z)))
                log_beta = z - neg_log1m
                if diagonal:
                    neg_log1m = jnp.where(causal, neg_log1m, 0.0)
                suffix = jnp.dot(neg_log1m.astype(BF16), later, preferred_element_type=F32)
                if carry is not None:
                    suffix = suffix + carry
                w = jnp.exp2(log_beta - suffix)
                if diagonal:
                    w = jnp.where(causal, w, 0.0)
                pv = jnp.dot(w.astype(BF16), v, preferred_element_type=F32)
                acc = pv if acc is None else acc + pv
                block_sum = jnp.sum(neg_log1m, axis=-1, keepdims=True)
                carry = block_sum if carry is None else carry + block_sum
            acc_scr[hd] = acc
            carry_scr[hd] = carry

    def stick_left():
        return jnp.min(carry_scr[...]) < SB_UNDERFLOW_LOG2

    @pl.when(qi == 0)
    def _():
        sweep([(qi, True)], first=True)

    @pl.when(qi > 0)
    def _():
        sweep([(qi, True), (qi - 1, False)], first=True)

    def cond(state):
        kb, go = state
        return jnp.logical_and(kb >= 0, go)

    def body(state):
        kb, _ = state
        sweep([(kb, False)], first=False)
        return kb - 1, stick_left()

    lax.while_loop(cond, body, (qi - 2, stick_left()))
    for hd in range(SB_HEADS):
        o_ref[:, hd * SB_HEAD_DIM:(hd + 1) * SB_HEAD_DIM] = acc_scr[hd].astype(BF16)


def _attention(proj, *, bsz, seq, tq=256):
    cols = proj.shape[1]
    proj3 = proj.reshape(bsz, seq, cols)
    w = SB_WIDTH
    out = pl.pallas_call(
        functools.partial(_attn_kernel, tq=tq),
        grid=(bsz, seq // tq),
        in_specs=[
            pl.BlockSpec((None, tq, w), lambda b, i: (b, i, 0)),
            pl.BlockSpec((None, seq, w), lambda b, i: (b, 0, 1), pipeline_mode=pl.Buffered(1)),
            pl.BlockSpec((None, seq, w), lambda b, i: (b, 0, 2), pipeline_mode=pl.Buffered(1)),
        ],
        out_specs=pl.BlockSpec((None, tq, w), lambda b, i: (b, i, 0)),
        out_shape=jax.ShapeDtypeStruct((bsz, seq, w), BF16),
        scratch_shapes=[pltpu.VMEM((SB_HEADS, tq, SB_HEAD_DIM), F32),
                        pltpu.VMEM((SB_HEADS, tq, 1), F32)],
        compiler_params=pltpu.CompilerParams(
            dimension_semantics=("parallel", "arbitrary"),
            vmem_limit_bytes=_vmem_limit(2 * seq * w * 2, 4 * tq * w * 2, 8 * SB_HEADS * tq * tq * 4)),
    )(proj3, proj3, proj3)
    return out.reshape(bsz * seq, w)


def _cv_kernel(a_ref, b_ref, ah_ref, bh_ref, wdw_ref, bdw_ref, lg_ref, lb_ref, o_ref,
               pad_scr, out_scr, *, tm, tps):
    i = pl.program_id(0)
    slabs = BRANCH_WIDTH // V7X_LANES
    lanes = lambda k: slice(k * V7X_LANES, (k + 1) * V7X_LANES)
    tile_rows = lambda first: pl.ds(first, V7X_SUBLANES_F32, stride=CV_STRIDE)

    glu = a_ref[...].astype(F32) * jax.nn.sigmoid(b_ref[...].astype(F32))
    for k in range(slabs):
        pad_scr[k, CV_HALO:, :] = glu[:, lanes(k)]

    @pl.when(i % tps == 0)
    def _():
        pad_scr[:, :CV_HALO, :] = jnp.zeros((slabs, CV_HALO, V7X_LANES), F32)

    @pl.when(i % tps != 0)
    def _():
        halo = ah_ref[...].astype(F32) * jax.nn.sigmoid(bh_ref[...].astype(F32))
        for k in range(slabs):
            pad_scr[k, :CV_HALO, :] = halo[:, lanes(k)]

    group_rows = CV_STRIDE * V7X_SUBLANES_F32

    def conv_group(g, _):
        for k in range(slabs):
            acc = [bdw_ref[:, lanes(k)]] * CV_STRIDE
            for tap in range(CV_FILTER):
                w_tap = wdw_ref[tap, :, lanes(k)]
                for s in range(CV_STRIDE):
                    src = g * group_rows + (s + CV_HALO - (CV_FILTER - 1) + tap)
                    acc[s] = acc[s] + w_tap * pad_scr[k, tile_rows(src), :]
            for s in range(CV_STRIDE):
                out_scr[k, tile_rows(g * group_rows + s), :] = acc[s]
        return _

    lax.fori_loop(0, tm // group_rows, conv_group, 0)
    y = _layer_norm(jnp.concatenate([out_scr[k] for k in range(slabs)], axis=1), lg_ref[...], lb_ref[...])
    o_ref[...] = (y * jax.nn.sigmoid(y)).astype(BF16)


def _conformer(proj, w_dw, b_dw, ln_g, ln_b, *, seq, tm=512):
    n = proj.shape[0]
    w = BRANCH_WIDTH
    tps = seq // tm
    col_a = 3 * SB_WIDTH // w
    col_b = col_a + 1
    hb = tm // CV_HALO
    row = lambda v: v.reshape(1, w)
    return pl.pallas_call(
        functools.partial(_cv_kernel, tm=tm, tps=tps),
        grid=(n // tm,),
        in_specs=[
            pl.BlockSpec((tm, w), lambda i: (i, col_a)),
            pl.BlockSpec((tm, w), lambda i: (i, col_b)),
            pl.BlockSpec((CV_HALO, w), lambda i: (jnp.maximum(i * hb - 1, 0), col_a)),
            pl.BlockSpec((CV_HALO, w), lambda i: (jnp.maximum(i * hb - 1, 0), col_b)),
            pl.BlockSpec((CV_FILTER, V7X_SUBLANES_F32, w), lambda i: (0, 0, 0)),
            pl.BlockSpec((V7X_SUBLANES_F32, w), lambda i: (0, 0)),
            pl.BlockSpec((1, w), lambda i: (0, 0)),
            pl.BlockSpec((1, w), lambda i: (0, 0)),
        ],
        out_specs=pl.BlockSpec((tm, w), lambda i: (i, 0)),
        out_shape=jax.ShapeDtypeStruct((n, w), BF16),
        scratch_shapes=[
            pltpu.VMEM((w // V7X_LANES, tm + CV_HALO, V7X_LANES), F32),
            pltpu.VMEM((w // V7X_LANES, tm, V7X_LANES), F32),
        ],
        compiler_params=pltpu.CompilerParams(
            dimension_semantics=("parallel",),
            vmem_limit_bytes=_vmem_limit(6 * tm * w * 2, 6 * (tm + CV_HALO) * w * 4)),
    )(proj, proj, proj, proj, jnp.broadcast_to(w_dw[:, None, :], (CV_FILTER, V7X_SUBLANES_F32, w)),
      jnp.broadcast_to(b_dw[None, :], (V7X_SUBLANES_F32, w)), row(ln_g), row(ln_b))


def _sg_kernel(u_ref, v_ref, lg_ref, lb_ref, ws_ref, bs_ref, o_ref, *, tm):
    t = SG_CHUNK
    u = jax.nn.gelu(u_ref[...].astype(F32))
    v = jax.nn.gelu(v_ref[...].astype(F32))
    vn = _layer_norm(v, lg_ref[...], lb_ref[...]).astype(BF16)
    pos_t = lax.broadcasted_iota(jnp.int32, (t, t), 0) // SG_CAUSAL_CHUNK
    pos_s = lax.broadcasted_iota(jnp.int32, (t, t), 1) // SG_CAUSAL_CHUNK
    chunk_causal = pos_s <= pos_t
    ws = [jnp.where(chunk_causal, ws_ref[g], 0.0).astype(BF16) for g in range(SG_GROUPS)]
    cg = BRANCH_WIDTH // SG_GROUPS
    first_group = lax.broadcasted_iota(jnp.int32, (t, V7X_LANES), 1) < cg
    for c in range(tm // t):
        rows = slice(c * t, (c + 1) * t)
        for p in range(BRANCH_WIDTH // V7X_LANES):
            cols = slice(p * V7X_LANES, (p + 1) * V7X_LANES)
            vb = vn[rows, cols]
            r0 = jnp.dot(ws[2 * p], vb, preferred_element_type=F32)
            r1 = jnp.dot(ws[2 * p + 1], vb, preferred_element_type=F32)
            mixed = jnp.where(first_group, r0, r1) + bs_ref[:, cols]
            o_ref[rows, cols] = (u[rows, cols] * mixed).astype(BF16)


def _spatial(proj, ln_g, ln_b, w_s, b_s, *, tm=512):
    n = proj.shape[0]
    w = BRANCH_WIDTH
    col_u = (3 * SB_WIDTH + 2 * w) // w
    col_v = col_u + 1
    bias = jnp.repeat(b_s.T, w // SG_GROUPS, axis=1)
    row = lambda v: v.reshape(1, w)
    return pl.pallas_call(
        functools.partial(_sg_kernel, tm=tm),
        grid=(n // tm,),
        in_specs=[
            pl.BlockSpec((tm, w), lambda i: (i, col_u)),
            pl.BlockSpec((tm, w), lambda i: (i, col_v)),
            pl.BlockSpec((1, w), lambda i: (0, 0)),
            pl.BlockSpec((1, w), lambda i: (0, 0)),
            pl.BlockSpec((SG_GROUPS, SG_CHUNK, SG_CHUNK), lambda i: (0, 0, 0)),
            pl.BlockSpec((SG_CHUNK, w), lambda i: (0, 0)),
        ],
        out_specs=pl.BlockSpec((tm, w), lambda i: (i, 0)),
        out_shape=jax.ShapeDtypeStruct((n, w), BF16),
        compiler_params=pltpu.CompilerParams(
            dimension_semantics=("parallel",),
            vmem_limit_bytes=_vmem_limit(6 * tm * w * 2, 8 * tm * w * 4)),
    )(proj, proj, row(ln_g), row(ln_b), w_s, bias)


def _merge_kernel(h_ref, ya_ref, yb_ref, yc_ref, wga_ref, wgb_ref, wgc_ref, bga_ref, bgb_ref, bgc_ref,
                  woa_ref, wob_ref, woc_ref, out_ref, wg_scr, wo_scr):
    @pl.when(pl.program_id(1) == 0)
    def _():
        for k, (wg_ref, wout_ref) in enumerate(((wga_ref, woa_ref), (wgb_ref, wob_ref), (wgc_ref, woc_ref))):
            wg_scr[k] = wg_ref[...].astype(BF16)
            wo_scr[k] = wout_ref[...].astype(BF16)

    for c in range(out_ref.shape[1] // MXU_COLS):
        cols = slice(c * MXU_COLS, (c + 1) * MXU_COLS)
        merged = None
        for k, (y_ref, bg_ref) in enumerate(((ya_ref, bga_ref), (yb_ref, bgb_ref), (yc_ref, bgc_ref))):
            gate = jax.nn.sigmoid(
                jnp.dot(h_ref[...], wg_scr[k, :, cols], preferred_element_type=F32) + bg_ref[:, cols])
            term = gate * jnp.dot(y_ref[...], wo_scr[k, :, cols], preferred_element_type=F32)
            merged = term if merged is None else merged + term
        out_ref[:, cols] = merged.astype(BF16)


def _merge(h, ya, yb, yc, w_gate, b_gate, wo_a, wo_b, wo_c, *, layer, tm=1024, tn=512):
    n, d = h.shape
    w = BRANCH_WIDTH
    nj = d // tn
    bg = b_gate[layer].reshape(1, 3 * d)
    once = pl.Buffered(1)
    gate_w = [pl.BlockSpec((None, d, tn), lambda j, i, k=k: (layer, 0, k * nj + j), pipeline_mode=once)
              for k in range(3)]
    gate_b = [pl.BlockSpec((1, tn), lambda j, i, k=k: (0, k * nj + j)) for k in range(3)]
    branch = pl.BlockSpec((tm, w), lambda j, i: (i, 0))
    branch_w = pl.BlockSpec((None, w, tn), lambda j, i: (layer, 0, j), pipeline_mode=once)
    return pl.pallas_call(
        _merge_kernel,
        grid=(nj, n // tm),
        in_specs=[
            pl.BlockSpec((tm, d), lambda j, i: (i, 0)),
            branch, branch, branch,
            *gate_w, *gate_b,
            branch_w, branch_w, branch_w,
        ],
        out_specs=pl.BlockSpec((tm, tn), lambda j, i: (i, j)),
        out_shape=jax.ShapeDtypeStruct((n, d), BF16),
        scratch_shapes=[pltpu.VMEM((3, d, tn), BF16), pltpu.VMEM((3, w, tn), BF16)],
        compiler_params=pltpu.CompilerParams(
            dimension_semantics=("arbitrary", "arbitrary"),
            vmem_limit_bytes=_vmem_limit(2 * tm * d * 2, 6 * tm * w * 2, 3 * d * tn * (4 + 2),
                                         3 * w * tn * (4 + 2), 2 * tm * tn * 2, 8 * tm * MXU_COLS * 4)),
    )(h, ya, yb, yc, w_gate, w_gate, w_gate, bg, bg, bg, wo_a, wo_b, wo_c)


def _residual_update(lhs_ref, x_ref, mod_ref, w_ref, gate_row, *out_refs):
    chunk = 2 * MXU_COLS
    for c in range(x_ref.shape[1] // chunk):
        cols = slice(c * chunk, (c + 1) * chunk)
        y = jnp.dot(lhs_ref[...], w_ref[:, cols], preferred_element_type=F32)
        xn = x_ref[:, cols] + mod_ref[gate_row:gate_row + 1, cols] * y
        for out_ref in out_refs:
            out_ref[:, cols] = xn


def _proj_residual_kernel(lhs_ref, x_ref, mod_ref, w_ref, xo_ref, *, gate_row):
    _residual_update(lhs_ref, x_ref, mod_ref, w_ref, gate_row, xo_ref)


def _alternate(step, even_scr, odd_scr):
    s = pl.program_id(0)

    @pl.when(s == 0)
    def _():
        odd_scr[...] = jnp.zeros_like(odd_scr)

    @pl.when(s % 2 == 0)
    def _():
        step(odd_scr, even_scr)

    @pl.when(s % 2 == 1)
    def _():
        step(even_scr, odd_scr)


def _proj_residual_norm_kernel(lhs_ref, x_ref, mod_ref, w_ref, g_ref, modp_ref, xo_ref, hn_ref,
                               even_scr, odd_scr, *, gate_row, shift_row, scale_row):
    def step(prev_scr, cur_scr):
        hn_ref[...] = _rms_modulate(prev_scr[...], g_ref[...], modp_ref[shift_row:shift_row + 1, :],
                                    modp_ref[scale_row:scale_row + 1, :]).astype(BF16)
        _residual_update(lhs_ref, x_ref, mod_ref, w_ref, gate_row, xo_ref, cur_scr)

    _alternate(step, even_scr, odd_scr)


def _proj_residual_final_kernel(lhs_ref, x_ref, mod_ref, w_ref, g_ref, yo_ref, even_scr, odd_scr, *, gate_row):
    def step(prev_scr, cur_scr):
        yo_ref[...] = _rms_norm(prev_scr[...], g_ref[...])
        _residual_update(lhs_ref, x_ref, mod_ref, w_ref, gate_row, cur_scr)

    _alternate(step, even_scr, odd_scr)


def _proj_residual(lhs, x, mod, w, *, layer, seq, gate_row, norm, g=None, shift_row=None, scale_row=None,
                   tm=512):
    n, d = x.shape
    kdim = lhs.shape[1]
    tps = seq // tm
    nt = n // tm
    cur = lambda s: jnp.minimum(s, nt - 1)
    prev = lambda s: jnp.maximum(s - 1, 0)
    in_specs = [
        pl.BlockSpec((tm, kdim), lambda s: (cur(s), 0)),
        pl.BlockSpec((tm, d), lambda s: (cur(s), 0)),
        pl.BlockSpec((None, 6, d), lambda s: (cur(s) // tps, 0, 0)),
        pl.BlockSpec((None, kdim, d), lambda s: (layer, 0, 0), pipeline_mode=pl.Buffered(1)),
    ]
    args = [lhs, x, mod, w]
    row_f32 = pl.BlockSpec((tm, d), lambda s: (cur(s), 0))
    vmem = [2 * tm * kdim * 2, 4 * tm * d * 4, kdim * d * 2, tm * d * 4]
    if norm is None:
        body = functools.partial(_proj_residual_kernel, gate_row=gate_row)
        grid, out_specs, scratch = (nt,), row_f32, []
        out_shape = jax.ShapeDtypeStruct((n, d), F32)
    else:
        grid = (nt + 1,)
        scratch = [pltpu.VMEM((tm, d), F32), pltpu.VMEM((tm, d), F32)]
        in_specs.append(pl.BlockSpec((1, d), lambda s: (0, 0)))
        args.append(g.reshape(1, d))
        vmem += [2 * tm * d * 4, 2 * tm * d * 4]
        if norm == "modulated":
            body = functools.partial(_proj_residual_norm_kernel, gate_row=gate_row,
                                     shift_row=shift_row, scale_row=scale_row)
            in_specs.append(pl.BlockSpec((None, 6, d), lambda s: (prev(s) // tps, 0, 0)))
            args.append(mod)
            out_specs = [row_f32, pl.BlockSpec((tm, d), lambda s: (prev(s), 0))]
            out_shape = [jax.ShapeDtypeStruct((n, d), F32), jax.ShapeDtypeStruct((n, d), BF16)]
        else:
            body = functools.partial(_proj_residual_final_kernel, gate_row=gate_row)
            out_specs = pl.BlockSpec((tm, d), lambda s: (prev(s), 0))
            out_shape = jax.ShapeDtypeStruct((n, d), F32)
    return pl.pallas_call(
        body, grid=grid, in_specs=in_specs, out_specs=out_specs, out_shape=out_shape,
        scratch_shapes=scratch,
        compiler_params=pltpu.CompilerParams(
            dimension_semantics=("arbitrary",), vmem_limit_bytes=_vmem_limit(*vmem)),
    )(*args)


def _ffn_up_kernel(h_ref, wg_ref, wv_ref, wdw_ref, bdw_ref, act_ref, gate_scr, val_scr, out_scr, w_scr,
                   *, tm, tps):
    i = pl.program_id(1)
    lanes = lambda k: slice(k * V7X_LANES, (k + 1) * V7X_LANES)
    tile_rows = lambda first: pl.ds(first, V7X_SUBLANES_F32, stride=CV_STRIDE)
    chunk_slabs = MXU_COLS // V7X_LANES

    @pl.when(i == 0)
    def _():
        w_scr[0] = wg_ref[...].astype(BF16)
        w_scr[1] = wv_ref[...].astype(BF16)

    @pl.when(i % tps == 0)
    def _():
        gate_scr[:, :FFN_HALO, :] = jnp.zeros((gate_scr.shape[0], FFN_HALO, V7X_LANES), F32)

    for c in range(gate_scr.shape[0] // chunk_slabs):
        cols = slice(c * MXU_COLS, (c + 1) * MXU_COLS)
        gate_pre = jnp.dot(h_ref[...], w_scr[0, :, cols], preferred_element_type=F32)
        val = jnp.dot(h_ref[...], w_scr[1, :, cols], preferred_element_type=F32)
        for k in range(c * chunk_slabs, (c + 1) * chunk_slabs):
            part = lanes(k - c * chunk_slabs)
            gate_scr[k, FFN_HALO:, :] = gate_pre[:, part]
            val_scr[k] = val[:, part]
            taps = [wdw_ref[tap, :, lanes(k)] for tap in range(FFN_FILTER)]
            bias = bdw_ref[:, lanes(k)]
            for first in [g * CV_STRIDE * V7X_SUBLANES_F32 + s
                          for g in range(tm // (CV_STRIDE * V7X_SUBLANES_F32)) for s in range(CV_STRIDE)]:
                conv = bias
                for tap in range(FFN_FILTER):
                    src = first + FFN_HALO - (FFN_FILTER - 1) + tap
                    conv = conv + taps[tap] * gate_scr[k, tile_rows(src), :]
                out_scr[k, tile_rows(first), :] = conv * jax.nn.sigmoid(conv) * val_scr[k, tile_rows(first), :]
            gate_scr[k, :FFN_HALO, :] = gate_scr[k, tm:tm + FFN_HALO, :]
        act_ref[:, cols] = jnp.concatenate(
            [out_scr[k] for k in range(c * chunk_slabs, (c + 1) * chunk_slabs)], axis=1).astype(BF16)


def _ffn_up(h, w_up, w_dw, b_dw, *, layer, seq, tm=1024, th=1024):
    n, d = h.shape
    f = w_up.shape[2] // 2
    tps = seq // tm
    nj = f // th
    return pl.pallas_call(
        functools.partial(_ffn_up_kernel, tm=tm, tps=tps),
        grid=(nj, n // tm),
        in_specs=[
            pl.BlockSpec((tm, d), lambda j, i: (i, 0)),
            pl.BlockSpec((None, d, th), lambda j, i: (layer, 0, j), pipeline_mode=pl.Buffered(1)),
            pl.BlockSpec((None, d, th), lambda j, i: (layer, 0, nj + j), pipeline_mode=pl.Buffered(1)),
            pl.BlockSpec((FFN_FILTER, V7X_SUBLANES_F32, th), lambda j, i: (0, 0, j)),
            pl.BlockSpec((V7X_SUBLANES_F32, th), lambda j, i: (0, j)),
        ],
        out_specs=pl.BlockSpec((tm, th), lambda j, i: (i, j)),
        out_shape=jax.ShapeDtypeStruct((n, f), BF16),
        scratch_shapes=[pltpu.VMEM((th // V7X_LANES, tm + FFN_HALO, V7X_LANES), F32),
                        pltpu.VMEM((th // V7X_LANES, tm, V7X_LANES), F32),
                        pltpu.VMEM((th // V7X_LANES, tm, V7X_LANES), F32),
                        pltpu.VMEM((2, d, th), BF16)],
        compiler_params=pltpu.CompilerParams(
            dimension_semantics=("arbitrary", "arbitrary"),
            vmem_limit_bytes=_vmem_limit(2 * tm * d * 2, 2 * d * th * (4 + 2), 2 * tm * th * 2,
                                         3 * (tm + FFN_HALO) * th * 4, 4 * tm * MXU_COLS * 4)),
    )(h, w_up, w_up, jnp.broadcast_to(w_dw[:, None, :], (FFN_FILTER, V7X_SUBLANES_F32, f)),
      jnp.broadcast_to(b_dw[None, :], (V7X_SUBLANES_F32, f)))


def kernel(x, c, w_ada, b_ada, g_norm1, g_norm2, w_in, sb_w_out, cv_w_dw, cv_b_dw, cv_ln_g, cv_ln_b,
           cv_w_out, sg_ln_g, sg_ln_b, sg_w_s, sg_b_s, sg_w_out, w_gate, b_gate, w_o, ffn_w_up,
           ffn_w_dw, ffn_b_dw, ffn_w_down, g_final):
    bsz, seq, d = x.shape
    depth = w_in.shape[0]
    in_cols = w_in.shape[2]
    xf = x.reshape(bsz * seq, d)
    mod = _ada_mod(c, w_ada, b_ada)

    q_scale = math.log2(math.e) / math.sqrt(SB_HEAD_DIM)
    col_scale = jnp.where(jnp.arange(in_cols) < SB_WIDTH, q_scale, 1.0).astype(F32)

    w_in_b = (w_in * col_scale).astype(BF16)
    w_o_b = w_o.astype(BF16)
    w_down_b = ffn_w_down.astype(BF16)

    for l in range(depth):
        proj, h = _inproj(xf, mod[l], g_norm1[l], w_in_b, layer=l, seq=seq)
        y_a = _attention(proj, bsz=bsz, seq=seq)
        y_b = _conformer(proj, cv_w_dw[l], cv_b_dw[l], cv_ln_g[l], cv_ln_b[l], seq=seq)
        y_c = _spatial(proj, sg_ln_g[l], sg_ln_b[l], sg_w_s[l], sg_b_s[l])
        merged = _merge(h, y_a, y_b, y_c, w_gate, b_gate, sb_w_out, cv_w_out, sg_w_out, layer=l)
        xf, h2 = _proj_residual(merged, xf, mod[l], w_o_b, layer=l, seq=seq, gate_row=GATE1,
                                norm="modulated", g=g_norm2[l], shift_row=SHIFT2, scale_row=SCALE2)
        act = _ffn_up(h2, ffn_w_up, ffn_w_dw[l], ffn_b_dw[l], layer=l, seq=seq)
        last = l == depth - 1
        xf = _proj_residual(act, xf, mod[l], w_down_b, layer=l, seq=seq, gate_row=GATE2,
                            norm="final" if last else None, g=g_final if last else None)
    return xf.reshape(bsz, seq, d)
```

```python
import functools
import math

import jax
import jax.numpy as jnp
from jax import lax
from jax.experimental import pallas as pl
from jax.experimental.pallas import tpu as pltpu

F32 = jnp.float32
BF16 = jnp.bfloat16

EPS = 1e-6
SB_HEADS = 4
SB_HEAD_DIM = 128
SB_WIDTH = SB_HEADS * SB_HEAD_DIM
BRANCH_WIDTH = 512
CV_FILTER = 31
SG_GROUPS = 8
SG_CHUNK = 128
SG_CAUSAL_CHUNK = 64
FFN_FILTER = 3

V7X_VMEM_BYTES = 64 * 1024 * 1024
V7X_LANES = 128
V7X_SUBLANES_F32 = 8
V7X_SUBLANES_BF16 = 16
MXU_COLS = 256

SB_UNDERFLOW_LOG2 = 150.0

CV_HALO = 2 * V7X_SUBLANES_BF16
CV_STRIDE = 4
FFN_HALO = V7X_SUBLANES_F32

SHIFT1, SCALE1, GATE1, SHIFT2, SCALE2, GATE2 = range(6)


COMPILER_TEMP_SHARE = 4
V7X_VMEM_RESERVE = 4 * 1024 * 1024


def _vmem_limit(*nbytes):
    need = sum(nbytes)
    return int(min(need + need // COMPILER_TEMP_SHARE, V7X_VMEM_BYTES - V7X_VMEM_RESERVE))


def _rms_norm(x, g):
    ms = jnp.mean(x * x, axis=-1, keepdims=True)
    return x * lax.rsqrt(ms + EPS) * g


def _rms_modulate(x, g, shift, scale):
    return _rms_norm(x, g * (1.0 + scale)) + shift


def _layer_norm(x, g, b):
    mu = jnp.mean(x, axis=-1, keepdims=True)
    xc = x - mu
    var = jnp.mean(xc * xc, axis=-1, keepdims=True)
    return xc * lax.rsqrt(var + EPS) * g + b


def _mod_kernel(c_ref, w_ref, b_ref, o_ref):
    c = c_ref[...]
    cond = (c * jax.nn.sigmoid(c)).astype(BF16)
    o_ref[...] = jnp.dot(cond, w_ref[...].astype(BF16), preferred_element_type=F32) + b_ref[...]


def _ada_mod(c, w_ada, b_ada):
    depth, d, cols = w_ada.shape
    bsz = c.shape[0]
    rows = V7X_SUBLANES_BF16
    tn = 1536
    c_pad = jnp.pad(c, ((0, rows - bsz), (0, 0)))
    out = pl.pallas_call(
        _mod_kernel,
        grid=(depth, cols // tn),
        in_specs=[
            pl.BlockSpec((rows, d), lambda l, j: (0, 0)),
            pl.BlockSpec((None, d, tn), lambda l, j: (l, 0, j)),
            pl.BlockSpec((None, 1, tn), lambda l, j: (l, 0, j)),
        ],
        out_specs=pl.BlockSpec((None, rows, tn), lambda l, j: (l, 0, j)),
        out_shape=jax.ShapeDtypeStruct((depth, rows, cols), F32),
        compiler_params=pltpu.CompilerParams(
            dimension_semantics=("parallel", "parallel"),
            vmem_limit_bytes=_vmem_limit(2 * d * tn * 4, d * tn * 2)),
    )(c_pad, w_ada, b_ada.reshape(depth, 1, cols))
    return out[:, :bsz].reshape(depth, bsz, 6, d)


def _inproj_kernel(x_ref, mod_ref, g_ref, w_ref, proj_ref, h_ref, h_even, h_odd):
    s = pl.program_id(0)

    @pl.when(s == 0)
    def _():
        h_odd[...] = jnp.zeros_like(h_odd)

    def step(h_cur, h_next):
        h = h_cur[...]
        h_ref[...] = h
        proj_ref[...] = jnp.dot(h, w_ref[...], preferred_element_type=F32).astype(BF16)
        h_next[...] = _rms_modulate(x_ref[...], g_ref[...], mod_ref[SHIFT1:SHIFT1 + 1, :],
                                    mod_ref[SCALE1:SCALE1 + 1, :]).astype(BF16)

    @pl.when(s % 2 == 0)
    def _():
        step(h_odd, h_even)

    @pl.when(s % 2 == 1)
    def _():
        step(h_even, h_odd)


def _inproj(x, mod, g, w, *, layer, seq, tm=512):
    n, d = x.shape
    cols = w.shape[2]
    tps = seq // tm
    nt = n // tm
    cur = lambda s: jnp.minimum(s, nt - 1)
    prev = lambda s: jnp.maximum(s - 1, 0)
    return pl.pallas_call(
        _inproj_kernel,
        grid=(nt + 1,),
        in_specs=[
            pl.BlockSpec((tm, d), lambda s: (cur(s), 0)),
            pl.BlockSpec((None, 6, d), lambda s: (cur(s) // tps, 0, 0)),
            pl.BlockSpec((1, d), lambda s: (0, 0)),
            pl.BlockSpec((None, d, cols), lambda s: (layer, 0, 0), pipeline_mode=pl.Buffered(1)),
        ],
        out_specs=[
            pl.BlockSpec((tm, cols), lambda s: (prev(s), 0)),
            pl.BlockSpec((tm, d), lambda s: (prev(s), 0)),
        ],
        out_shape=[
            jax.ShapeDtypeStruct((n, cols), BF16),
            jax.ShapeDtypeStruct((n, d), BF16),
        ],
        scratch_shapes=[pltpu.VMEM((tm, d), BF16), pltpu.VMEM((tm, d), BF16)],
        compiler_params=pltpu.CompilerParams(
            dimension_semantics=("arbitrary",),
            vmem_limit_bytes=_vmem_limit(2 * tm * d * 4, d * cols * 2, 2 * tm * cols * 2,
                                         4 * tm * d * 2, tm * cols * 4)),
    )(x, mod, g.reshape(1, d), w)


def _attn_kernel(q_ref, k_ref, v_ref, o_ref, acc_scr, carry_scr, *, tq):
    qi = pl.program_id(1)
    row = lax.broadcasted_iota(jnp.int32, (tq, tq), 0)
    col = lax.broadcasted_iota(jnp.int32, (tq, tq), 1)
    later = jnp.where(row > col, 1.0, 0.0).astype(BF16)
    causal = col < row

    def sweep(blocks, first):
        for hd in range(SB_HEADS):
            cols = slice(hd * SB_HEAD_DIM, (hd + 1) * SB_HEAD_DIM)
            q = q_ref[:, cols]
            carry = None if first else carry_scr[hd]
            acc = None if first else acc_scr[hd]
            for kb, diagonal in blocks:
                start = pl.multiple_of(kb * tq, tq)
                k = k_ref[pl.ds(start, tq), cols]
                v = v_ref[pl.ds(start, tq), cols]
                z = lax.dot_general(q, k, (((1,), (1,)), ((), ())), preferred_element_type=F32)
                neg_log1m = jnp.maximum(z, 0.0) + jnp.log2(1.0 + jnp.exp2(-jnp.abs(z)))
                log_beta = z - neg_log1m
                if diagonal:
                    neg_log1m = jnp.where(causal, neg_log1m, 0.0)
                suffix = jnp.dot(neg_log1m.astype(BF16), later, preferred_element_type=F32)
                if carry is not None:
                    suffix = suffix + carry
                w = jnp.exp2(log_beta - suffix)
                if diagonal:
                    w = jnp.where(causal, w, 0.0)
                pv = jnp.dot(w.astype(BF16), v, preferred_element_type=F32)
                acc = pv if acc is None else acc + pv
                block_sum = jnp.sum(neg_log1m, axis=-1, keepdims=True)
                carry = block_sum if carry is None else carry + block_sum
            acc_scr[hd] = acc
            carry_scr[hd] = carry

    def stick_left():
        return jnp.min(carry_scr[...]) < SB_UNDERFLOW_LOG2

    @pl.when(qi == 0)
    def _():
        sweep([(qi, True)], first=True)

    @pl.when(qi > 0)
    def _():
        sweep([(qi, True), (qi - 1, False)], first=True)

    def cond(state):
        kb, go = state
        return jnp.logical_and(kb >= 0, go)

    def body(state):
        kb, _ = state
        sweep([(kb, False)], first=False)
        return kb - 1, stick_left()

    lax.while_loop(cond, body, (qi - 2, stick_left()))
    for hd in range(SB_HEADS):
        o_ref[:, hd * SB_HEAD_DIM:(hd + 1) * SB_HEAD_DIM] = acc_scr[hd].astype(BF16)


def _attention(proj, *, bsz, seq, tq=256):
    cols = proj.shape[1]
    proj3 = proj.reshape(bsz, seq, cols)
    w = SB_WIDTH
    out = pl.pallas_call(
        functools.partial(_attn_kernel, tq=tq),
        grid=(bsz, seq // tq),
        in_specs=[
            pl.BlockSpec((None, tq, w), lambda b, i: (b, i, 0)),
            pl.BlockSpec((None, seq, w), lambda b, i: (b, 0, 1), pipeline_mode=pl.Buffered(1)),
            pl.BlockSpec((None, seq, w), lambda b, i: (b, 0, 2), pipeline_mode=pl.Buffered(1)),
        ],
        out_specs=pl.BlockSpec((None, tq, w), lambda b, i: (b, i, 0)),
        out_shape=jax.ShapeDtypeStruct((bsz, seq, w), BF16),
        scratch_shapes=[pltpu.VMEM((SB_HEADS, tq, SB_HEAD_DIM), F32),
                        pltpu.VMEM((SB_HEADS, tq, 1), F32)],
        compiler_params=pltpu.CompilerParams(
            dimension_semantics=("parallel", "arbitrary"),
            vmem_limit_bytes=_vmem_limit(2 * seq * w * 2, 4 * tq * w * 2, 8 * SB_HEADS * tq * tq * 4)),
    )(proj3, proj3, proj3)
    return out.reshape(bsz * seq, w)


def _cv_kernel(a_ref, b_ref, ah_ref, bh_ref, wdw_ref, bdw_ref, lg_ref, lb_ref, o_ref,
               pad_scr, out_scr, *, tm, tps):
    i = pl.program_id(0)
    slabs = BRANCH_WIDTH // V7X_LANES
    lanes = lambda k: slice(k * V7X_LANES, (k + 1) * V7X_LANES)
    tile_rows = lambda first: pl.ds(first, V7X_SUBLANES_F32, stride=CV_STRIDE)

    glu = a_ref[...].astype(F32) * jax.nn.sigmoid(b_ref[...].astype(F32))
    for k in range(slabs):
        pad_scr[k, CV_HALO:, :] = glu[:, lanes(k)]

    @pl.when(i % tps == 0)
    def _():
        pad_scr[:, :CV_HALO, :] = jnp.zeros((slabs, CV_HALO, V7X_LANES), F32)

    @pl.when(i % tps != 0)
    def _():
        halo = ah_ref[...].astype(F32) * jax.nn.sigmoid(bh_ref[...].astype(F32))
        for k in range(slabs):
            pad_scr[k, :CV_HALO, :] = halo[:, lanes(k)]

    group_rows = CV_STRIDE * V7X_SUBLANES_F32

    def conv_group(g, _):
        for k in range(slabs):
            acc = [bdw_ref[:, lanes(k)]] * CV_STRIDE
            for tap in range(CV_FILTER):
                w_tap = wdw_ref[tap, :, lanes(k)]
                for s in range(CV_STRIDE):
                    src = g * group_rows + (s + CV_HALO - (CV_FILTER - 1) + tap)
                    acc[s] = acc[s] + w_tap * pad_scr[k, tile_rows(src), :]
            for s in range(CV_STRIDE):
                out_scr[k, tile_rows(g * group_rows + s), :] = acc[s]
        return _

    lax.fori_loop(0, tm // group_rows, conv_group, 0)
    y = _layer_norm(jnp.concatenate([out_scr[k] for k in range(slabs)], axis=1), lg_ref[...], lb_ref[...])
    o_ref[...] = (y * jax.nn.sigmoid(y)).astype(BF16)


def _conformer(proj, w_dw, b_dw, ln_g, ln_b, *, seq, tm=512):
    n = proj.shape[0]
    w = BRANCH_WIDTH
    tps = seq // tm
    col_a = 3 * SB_WIDTH // w
    col_b = col_a + 1
    hb = tm // CV_HALO
    row = lambda v: v.reshape(1, w)
    return pl.pallas_call(
        functools.partial(_cv_kernel, tm=tm, tps=tps),
        grid=(n // tm,),
        in_specs=[
            pl.BlockSpec((tm, w), lambda i: (i, col_a)),
            pl.BlockSpec((tm, w), lambda i: (i, col_b)),
            pl.BlockSpec((CV_HALO, w), lambda i: (jnp.maximum(i * hb - 1, 0), col_a)),
            pl.BlockSpec((CV_HALO, w), lambda i: (jnp.maximum(i * hb - 1, 0), col_b)),
            pl.BlockSpec((CV_FILTER, V7X_SUBLANES_F32, w), lambda i: (0, 0, 0)),
            pl.BlockSpec((V7X_SUBLANES_F32, w), lambda i: (0, 0)),
            pl.BlockSpec((1, w), lambda i: (0, 0)),
            pl.BlockSpec((1, w), lambda i: (0, 0)),
        ],
        out_specs=pl.BlockSpec((tm, w), lambda i: (i, 0)),
        out_shape=jax.ShapeDtypeStruct((n, w), BF16),
        scratch_shapes=[
            pltpu.VMEM((w // V7X_LANES, tm + CV_HALO, V7X_LANES), F32),
            pltpu.VMEM((w // V7X_LANES, tm, V7X_LANES), F32),
        ],
        compiler_params=pltpu.CompilerParams(
            dimension_semantics=("parallel",),
            vmem_limit_bytes=_vmem_limit(6 * tm * w * 2, 6 * (tm + CV_HALO) * w * 4)),
    )(proj, proj, proj, proj, jnp.broadcast_to(w_dw[:, None, :], (CV_FILTER, V7X_SUBLANES_F32, w)),
      jnp.broadcast_to(b_dw[None, :], (V7X_SUBLANES_F32, w)), row(ln_g), row(ln_b))


def _sg_kernel(u_ref, v_ref, lg_ref, lb_ref, ws_ref, bs_ref, o_ref, *, tm):
    t = SG_CHUNK
    u = jax.nn.gelu(u_ref[...].astype(F32))
    v = jax.nn.gelu(v_ref[...].astype(F32))
    vn = _layer_norm(v, lg_ref[...], lb_ref[...]).astype(BF16)
    pos_t = lax.broadcasted_iota(jnp.int32, (t, t), 0) // SG_CAUSAL_CHUNK
    pos_s = lax.broadcasted_iota(jnp.int32, (t, t), 1) // SG_CAUSAL_CHUNK
    chunk_causal = pos_s <= pos_t
    ws = [jnp.where(chunk_causal, ws_ref[g], 0.0).astype(BF16) for g in range(SG_GROUPS)]
    cg = BRANCH_WIDTH // SG_GROUPS
    first_group = lax.broadcasted_iota(jnp.int32, (t, V7X_LANES), 1) < cg
    for c in range(tm // t):
        rows = slice(c * t, (c + 1) * t)
        for p in range(BRANCH_WIDTH // V7X_LANES):
            cols = slice(p * V7X_LANES, (p + 1) * V7X_LANES)
            vb = vn[rows, cols]
            r0 = jnp.dot(ws[2 * p], vb, preferred_element_type=F32)
            r1 = jnp.dot(ws[2 * p + 1], vb, preferred_element_type=F32)
            mixed = jnp.where(first_group, r0, r1) + bs_ref[:, cols]
            o_ref[rows, cols] = (u[rows, cols] * mixed).astype(BF16)


def _spatial(proj, ln_g, ln_b, w_s, b_s, *, tm=512):
    n = proj.shape[0]
    w = BRANCH_WIDTH
    col_u = (3 * SB_WIDTH + 2 * w) // w
    col_v = col_u + 1
    bias = jnp.repeat(b_s.T, w // SG_GROUPS, axis=1)
    row = lambda v: v.reshape(1, w)
    return pl.pallas_call(
        functools.partial(_sg_kernel, tm=tm),
        grid=(n // tm,),
        in_specs=[
            pl.BlockSpec((tm, w), lambda i: (i, col_u)),
            pl.BlockSpec((tm, w), lambda i: (i, col_v)),
            pl.BlockSpec((1, w), lambda i: (0, 0)),
            pl.BlockSpec((1, w), lambda i: (0, 0)),
            pl.BlockSpec((SG_GROUPS, SG_CHUNK, SG_CHUNK), lambda i: (0, 0, 0)),
            pl.BlockSpec((SG_CHUNK, w), lambda i: (0, 0)),
        ],
        out_specs=pl.BlockSpec((tm, w), lambda i: (i, 0)),
        out_shape=jax.ShapeDtypeStruct((n, w), BF16),
        compiler_params=pltpu.CompilerParams(
            dimension_semantics=("parallel",),
            vmem_limit_bytes=_vmem_limit(6 * tm * w * 2, 8 * tm * w * 4)),
    )(proj, proj, row(ln_g), row(ln_b), w_s, bias)


def _merge_kernel(h_ref, ya_ref, yb_ref, yc_ref, wga_ref, wgb_ref, wgc_ref, bga_ref, bgb_ref, bgc_ref,
                  woa_ref, wob_ref, woc_ref, out_ref, wg_scr, wo_scr):
    @pl.when(pl.program_id(1) == 0)
    def _():
        for k, (wg_ref, wout_ref) in enumerate(((wga_ref, woa_ref), (wgb_ref, wob_ref), (wgc_ref, woc_ref))):
            wg_scr[k] = wg_ref[...].astype(BF16)
            wo_scr[k] = wout_ref[...].astype(BF16)

    for c in range(out_ref.shape[1] // MXU_COLS):
        cols = slice(c * MXU_COLS, (c + 1) * MXU_COLS)
        merged = None
        for k, (y_ref, bg_ref) in enumerate(((ya_ref, bga_ref), (yb_ref, bgb_ref), (yc_ref, bgc_ref))):
            gate = jax.nn.sigmoid(
                jnp.dot(h_ref[...], wg_scr[k, :, cols], preferred_element_type=F32) + bg_ref[:, cols])
            term = gate * jnp.dot(y_ref[...], wo_scr[k, :, cols], preferred_element_type=F32)
            merged = term if merged is None else merged + term
        out_ref[:, cols] = merged.astype(BF16)


def _merge(h, ya, yb, yc, w_gate, b_gate, wo_a, wo_b, wo_c, *, layer, tm=1024, tn=512):
    n, d = h.shape
    w = BRANCH_WIDTH
    nj = d // tn
    bg = b_gate[layer].reshape(1, 3 * d)
    once = pl.Buffered(1)
    gate_w = [pl.BlockSpec((None, d, tn), lambda j, i, k=k: (layer, 0, k * nj + j), pipeline_mode=once)
              for k in range(3)]
    gate_b = [pl.BlockSpec((1, tn), lambda j, i, k=k: (0, k * nj + j)) for k in range(3)]
    branch = pl.BlockSpec((tm, w), lambda j, i: (i, 0))
    branch_w = pl.BlockSpec((None, w, tn), lambda j, i: (layer, 0, j), pipeline_mode=once)
    return pl.pallas_call(
        _merge_kernel,
        grid=(nj, n // tm),
        in_specs=[
            pl.BlockSpec((tm, d), lambda j, i: (i, 0)),
            branch, branch, branch,
            *gate_w, *gate_b,
            branch_w, branch_w, branch_w,
        ],
        out_specs=pl.BlockSpec((tm, tn), lambda j, i: (i, j)),
        out_shape=jax.ShapeDtypeStruct((n, d), BF16),
        scratch_shapes=[pltpu.VMEM((3, d, tn), BF16), pltpu.VMEM((3, w, tn), BF16)],
        compiler_params=pltpu.CompilerParams(
            dimension_semantics=("arbitrary", "arbitrary"),
            vmem_limit_bytes=_vmem_limit(2 * tm * d * 2, 6 * tm * w * 2, 3 * d * tn * (4 + 2),
                                         3 * w * tn * (4 + 2), 2 * tm * tn * 2, 8 * tm * MXU_COLS * 4)),
    )(h, ya, yb, yc, w_gate, w_gate, w_gate, bg, bg, bg, wo_a, wo_b, wo_c)


def _residual_update(lhs_ref, x_ref, mod_ref, w_ref, gate_row, *out_refs):
    chunk = 2 * MXU_COLS
    for c in range(x_ref.shape[1] // chunk):
        cols = slice(c * chunk, (c + 1) * chunk)
        y = jnp.dot(lhs_ref[...], w_ref[:, cols], preferred_element_type=F32)
        xn = x_ref[:, cols] + mod_ref[gate_row:gate_row + 1, cols] * y
        for out_ref in out_refs:
            out_ref[:, cols] = xn


def _proj_residual_kernel(lhs_ref, x_ref, mod_ref, w_ref, xo_ref, *, gate_row):
    _residual_update(lhs_ref, x_ref, mod_ref, w_ref, gate_row, xo_ref)


def _alternate(step, even_scr, odd_scr):
    s = pl.program_id(0)

    @pl.when(s == 0)
    def _():
        odd_scr[...] = jnp.zeros_like(odd_scr)

    @pl.when(s % 2 == 0)
    def _():
        step(odd_scr, even_scr)

    @pl.when(s % 2 == 1)
    def _():
        step(even_scr, odd_scr)


def _proj_residual_norm_kernel(lhs_ref, x_ref, mod_ref, w_ref, g_ref, modp_ref, xo_ref, hn_ref,
                               even_scr, odd_scr, *, gate_row, shift_row, scale_row):
    def step(prev_scr, cur_scr):
        hn_ref[...] = _rms_modulate(prev_scr[...], g_ref[...], modp_ref[shift_row:shift_row + 1, :],
                                    modp_ref[scale_row:scale_row + 1, :]).astype(BF16)
        _residual_update(lhs_ref, x_ref, mod_ref, w_ref, gate_row, xo_ref, cur_scr)

    _alternate(step, even_scr, odd_scr)


def _proj_residual_final_kernel(lhs_ref, x_ref, mod_ref, w_ref, g_ref, yo_ref, even_scr, odd_scr, *, gate_row):
    def step(prev_scr, cur_scr):
        yo_ref[...] = _rms_norm(prev_scr[...], g_ref[...])
        _residual_update(lhs_ref, x_ref, mod_ref, w_ref, gate_row, cur_scr)

    _alternate(step, even_scr, odd_scr)


def _proj_residual(lhs, x, mod, w, *, layer, seq, gate_row, norm, g=None, shift_row=None, scale_row=None,
                   tm=512):
    n, d = x.shape
    kdim = lhs.shape[1]
    tps = seq // tm
    nt = n // tm
    cur = lambda s: jnp.minimum(s, nt - 1)
    prev = lambda s: jnp.maximum(s - 1, 0)
    in_specs = [
        pl.BlockSpec((tm, kdim), lambda s: (cur(s), 0)),
        pl.BlockSpec((tm, d), lambda s: (cur(s), 0)),
        pl.BlockSpec((None, 6, d), lambda s: (cur(s) // tps, 0, 0)),
        pl.BlockSpec((None, kdim, d), lambda s: (layer, 0, 0), pipeline_mode=pl.Buffered(1)),
    ]
    args = [lhs, x, mod, w]
    row_f32 = pl.BlockSpec((tm, d), lambda s: (cur(s), 0))
    vmem = [2 * tm * kdim * 2, 4 * tm * d * 4, kdim * d * 2, tm * d * 4]
    if norm is None:
        body = functools.partial(_proj_residual_kernel, gate_row=gate_row)
        grid, out_specs, scratch = (nt,), row_f32, []
        out_shape = jax.ShapeDtypeStruct((n, d), F32)
    else:
        grid = (nt + 1,)
        scratch = [pltpu.VMEM((tm, d), F32), pltpu.VMEM((tm, d), F32)]
        in_specs.append(pl.BlockSpec((1, d), lambda s: (0, 0)))
        args.append(g.reshape(1, d))
        vmem += [2 * tm * d * 4, 2 * tm * d * 4]
        if norm == "modulated":
            body = functools.partial(_proj_residual_norm_kernel, gate_row=gate_row,
                                     shift_row=shift_row, scale_row=scale_row)
            in_specs.append(pl.BlockSpec((None, 6, d), lambda s: (prev(s) // tps, 0, 0)))
            args.append(mod)
            out_specs = [row_f32, pl.BlockSpec((tm, d), lambda s: (prev(s), 0))]
            out_shape = [jax.ShapeDtypeStruct((n, d), F32), jax.ShapeDtypeStruct((n, d), BF16)]
        else:
            body = functools.partial(_proj_residual_final_kernel, gate_row=gate_row)
            out_specs = pl.BlockSpec((tm, d), lambda s: (prev(s), 0))
            out_shape = jax.ShapeDtypeStruct((n, d), F32)
    return pl.pallas_call(
        body, grid=grid, in_specs=in_specs, out_specs=out_specs, out_shape=out_shape,
        scratch_shapes=scratch,
        compiler_params=pltpu.CompilerParams(
            dimension_semantics=("arbitrary",), vmem_limit_bytes=_vmem_limit(*vmem)),
    )(*args)


def _ffn_up_kernel(h_ref, wg_ref, wv_ref, wdw_ref, bdw_ref, act_ref, gate_scr, val_scr, out_scr, w_scr,
                   *, tm, tps):
    i = pl.program_id(1)
    lanes = lambda k: slice(k * V7X_LANES, (k + 1) * V7X_LANES)
    tile_rows = lambda first: pl.ds(first, V7X_SUBLANES_F32, stride=CV_STRIDE)
    chunk_slabs = 2 * MXU_COLS // V7X_LANES

    @pl.when(i == 0)
    def _():
        w_scr[0] = wg_ref[...].astype(BF16)
        w_scr[1] = wv_ref[...].astype(BF16)

    @pl.when(i % tps == 0)
    def _():
        gate_scr[:, :FFN_HALO, :] = jnp.zeros((gate_scr.shape[0], FFN_HALO, V7X_LANES), F32)

    for c in range(gate_scr.shape[0] // chunk_slabs):
        cols = slice(c * chunk_slabs * V7X_LANES, (c + 1) * chunk_slabs * V7X_LANES)
        gate_pre = jnp.dot(h_ref[...], w_scr[0, :, cols], preferred_element_type=F32)
        val = jnp.dot(h_ref[...], w_scr[1, :, cols], preferred_element_type=F32)
        for k in range(c * chunk_slabs, (c + 1) * chunk_slabs):
            part = lanes(k - c * chunk_slabs)
            gate_scr[k, FFN_HALO:, :] = gate_pre[:, part]
            val_scr[k] = val[:, part]
            taps = [wdw_ref[tap, :, lanes(k)] for tap in range(FFN_FILTER)]
            bias = bdw_ref[:, lanes(k)]
            for first in [g * CV_STRIDE * V7X_SUBLANES_F32 + s
                          for g in range(tm // (CV_STRIDE * V7X_SUBLANES_F32)) for s in range(CV_STRIDE)]:
                conv = bias
                for tap in range(FFN_FILTER):
                    src = first + FFN_HALO - (FFN_FILTER - 1) + tap
                    conv = conv + taps[tap] * gate_scr[k, tile_rows(src), :]
                out_scr[k, tile_rows(first), :] = conv * jax.nn.sigmoid(conv) * val_scr[k, tile_rows(first), :]
            gate_scr[k, :FFN_HALO, :] = gate_scr[k, tm:tm + FFN_HALO, :]
        act_ref[:, cols] = jnp.concatenate(
            [out_scr[k] for k in range(c * chunk_slabs, (c + 1) * chunk_slabs)], axis=1).astype(BF16)


def _ffn_up(h, w_up, w_dw, b_dw, *, layer, seq, tm=1024, th=1024):
    n, d = h.shape
    f = w_up.shape[2] // 2
    tps = seq // tm
    nj = f // th
    return pl.pallas_call(
        functools.partial(_ffn_up_kernel, tm=tm, tps=tps),
        grid=(nj, n // tm),
        in_specs=[
            pl.BlockSpec((tm, d), lambda j, i: (i, 0)),
            pl.BlockSpec((None, d, th), lambda j, i: (layer, 0, j), pipeline_mode=pl.Buffered(1)),
            pl.BlockSpec((None, d, th), lambda j, i: (layer, 0, nj + j), pipeline_mode=pl.Buffered(1)),
            pl.BlockSpec((FFN_FILTER, V7X_SUBLANES_F32, th), lambda j, i: (0, 0, j)),
            pl.BlockSpec((V7X_SUBLANES_F32, th), lambda j, i: (0, j)),
        ],
        out_specs=pl.BlockSpec((tm, th), lambda j, i: (i, j)),
        out_shape=jax.ShapeDtypeStruct((n, f), BF16),
        scratch_shapes=[pltpu.VMEM((th // V7X_LANES, tm + FFN_HALO, V7X_LANES), F32),
                        pltpu.VMEM((th // V7X_LANES, tm, V7X_LANES), F32),
                        pltpu.VMEM((th // V7X_LANES, tm, V7X_LANES), F32),
                        pltpu.VMEM((2, d, th), BF16)],
        compiler_params=pltpu.CompilerParams(
            dimension_semantics=("arbitrary", "arbitrary"),
            vmem_limit_bytes=_vmem_limit(2 * tm * d * 2, 2 * d * th * (4 + 2), 2 * tm * th * 2,
                                         3 * (tm + FFN_HALO) * th * 4, 4 * tm * MXU_COLS * 4)),
    )(h, w_up, w_up, jnp.broadcast_to(w_dw[:, None, :], (FFN_FILTER, V7X_SUBLANES_F32, f)),
      jnp.broadcast_to(b_dw[None, :], (V7X_SUBLANES_F32, f)))


def kernel(x, c, w_ada, b_ada, g_norm1, g_norm2, w_in, sb_w_out, cv_w_dw, cv_b_dw, cv_ln_g, cv_ln_b,
           cv_w_out, sg_ln_g, sg_ln_b, sg_w_s, sg_b_s, sg_w_out, w_gate, b_gate, w_o, ffn_w_up,
           ffn_w_dw, ffn_b_dw, ffn_w_down, g_final):
    bsz, seq, d = x.shape
    depth = w_in.shape[0]
    in_cols = w_in.shape[2]
    xf = x.reshape(bsz * seq, d)
    mod = _ada_mod(c, w_ada, b_ada)

    q_scale = math.log2(math.e) / math.sqrt(SB_HEAD_DIM)
    col_scale = jnp.where(jnp.arange(in_cols) < SB_WIDTH, q_scale, 1.0).astype(F32)

    w_in_b = (w_in * col_scale).astype(BF16)
    w_o_b = w_o.astype(BF16)
    w_down_b = ffn_w_down.astype(BF16)

    for l in range(depth):
        proj, h = _inproj(xf, mod[l], g_norm1[l], w_in_b, layer=l, seq=seq)
        y_a = _attention(proj, bsz=bsz, seq=seq)
        y_b = _conformer(proj, cv_w_dw[l], cv_b_dw[l], cv_ln_g[l], cv_ln_b[l], seq=seq)
        y_c = _spatial(proj, sg_ln_g[l], sg_ln_b[l], sg_w_s[l], sg_b_s[l])
        merged = _merge(h, y_a, y_b, y_c, w_gate, b_gate, sb_w_out, cv_w_out, sg_w_out, layer=l)
        xf, h2 = _proj_residual(merged, xf, mod[l], w_o_b, layer=l, seq=seq, gate_row=GATE1,
                                norm="modulated", g=g_norm2[l], shift_row=SHIFT2, scale_row=SCALE2)
        act = _ffn_up(h2, ffn_w_up, ffn_w_dw[l], ffn_b_dw[l], layer=l, seq=seq)
        last = l == depth - 1
        xf = _proj_residual(act, xf, mod[l], w_down_b, layer=l, seq=seq, gate_row=GATE2,
                            norm="final" if last else None, g=g_final if last else None)
    return xf.reshape(bsz, seq, d)
```
